```python
import jax, jax.numpy as jnp
from jax import lax
import numpy as np

D_MODEL = 2048
BATCH = 4
SEQ = 2048
DEPTH = 2
DEC_BATCH = 128
DEC_SEQ = 8
PAST_LEN = 8192
PAGE_SIZE = 128

N_HEADS = 16
QK_NOPE = 128
QK_ROPE = 64
V_HEAD = 128
Q_LORA = 512
KV_LORA = 512
ROPE_THETA = 10000.0
Q_BLOCK = 128
CONV_CH = D_MODEL
CONV_K = 31
N_MEM = 256
XA_HEADS = 4
XA_HEAD_DIM = 128
N_GROUPS = 4
EXP_PER_GROUP = 4
N_EXPERTS = N_GROUPS * EXP_PER_GROUP
TOP_K_IN_GROUP = 2
D_EXPERT = 512
LN_EPS = 1e-5
RMS_EPS = 1e-6
DEEPNORM_ALPHA = (2 * DEPTH) ** 0.25
DEEPNORM_BETA = (8 * DEPTH) ** -0.25
NEG_INF = -1e30
P_IN = Q_LORA + KV_LORA + QK_ROPE + 2 * CONV_CH + 2 * D_MODEL
SPLITS = (Q_LORA, Q_LORA + KV_LORA, Q_LORA + KV_LORA + QK_ROPE, Q_LORA + KV_LORA + QK_ROPE + 2 * CONV_CH)

kernel_name = "hybrid_mla_conformer_hmoe_decode_step"


def rms_norm(x, g):
    xf = x.astype(jnp.float32)
    y = xf * lax.rsqrt(jnp.mean(xf * xf, axis=-1, keepdims=True) + RMS_EPS)
    return (y * g.astype(jnp.float32)).astype(x.dtype)


def layer_norm(x, g, b):
    xf = x.astype(jnp.float32)
    mu = jnp.mean(xf, axis=-1, keepdims=True)
    var = jnp.mean(jnp.square(xf - mu), axis=-1, keepdims=True)
    y = (xf - mu) * lax.rsqrt(var + LN_EPS) * g.astype(jnp.float32) + b.astype(jnp.float32)
    return y.astype(x.dtype)


def rope(x, pos):
    half = QK_ROPE // 2
    inv = ROPE_THETA ** (-jnp.arange(half, dtype=jnp.float32) / half)
    ang = pos.astype(jnp.float32)[:, None] * inv[None, :]
    cos = jnp.cos(ang)[None, :, None, :]
    sin = jnp.sin(ang)[None, :, None, :]
    xf = x.astype(jnp.float32)
    x1, x2 = xf[..., :half], xf[..., half:]
    return jnp.concatenate([x1 * cos - x2 * sin, x2 * cos + x1 * sin], axis=-1).astype(x.dtype)


def mixer_projections(x, pos, p):
    proj = jnp.einsum('btd,dp->btp', x, p['w_in'])
    q_c, kv_c, k_r, conv_in, gate_pre = jnp.split(proj, list(SPLITS), axis=-1)
    q = jnp.einsum('btr,rhe->bthe', rms_norm(q_c, p['q_norm_g']), p['w_uq'])
    q_nope = q[..., :QK_NOPE]
    q_rope = rope(q[..., QK_NOPE:], pos)
    q_lat = jnp.einsum('bthn,chn->bthc', q_nope, p['w_uk'])
    ckv = rms_norm(kv_c, p['kv_norm_g'])
    krope = rope(k_r[:, :, None, :], pos)[:, :, 0]
    glu = conv_in[..., :CONV_CH] * jax.nn.sigmoid(conv_in[..., CONV_CH:])
    gates = jax.nn.sigmoid(gate_pre + p['b_gate'])
    return q_lat, q_rope, ckv, krope, glu, gates


def mla_prompt(q_lat, q_rope, ckv, krope):
    B, S = ckv.shape[:2]
    scale = (QK_NOPE + QK_ROPE) ** -0.5
    kpos = jnp.arange(S)

    def block(i):
        ql = lax.dynamic_slice_in_dim(q_lat, i * Q_BLOCK, Q_BLOCK, axis=1)
        qr = lax.dynamic_slice_in_dim(q_rope, i * Q_BLOCK, Q_BLOCK, axis=1)
        s = (jnp.einsum('bqhc,bkc->bhqk', ql, ckv)
             + jnp.einsum('bqhr,bkr->bhqk', qr, krope)).astype(jnp.float32) * scale
        qpos = i * Q_BLOCK + jnp.arange(Q_BLOCK)
        s = jnp.where(kpos[None, :] <= qpos[:, None], s, NEG_INF)
        prob = jax.nn.softmax(s, axis=-1).astype(ckv.dtype)
        return jnp.einsum('bhqk,bkc->bqhc', prob, ckv)

    out = lax.map(block, jnp.arange(S // Q_BLOCK))
    return jnp.moveaxis(out, 0, 1).reshape(B, S, N_HEADS, KV_LORA)


def mla_sample(q_lat, q_rope, ckv_new, krope_new, ckv_past, krope_past):
    T = ckv_new.shape[1]
    past = ckv_past.shape[1]
    scale = (QK_NOPE + QK_ROPE) ** -0.5
    s_past = jnp.einsum('bqhc,bkc->bhqk', q_lat, ckv_past) + jnp.einsum('bqhr,bkr->bhqk', q_rope, krope_past)
    s_new = jnp.einsum('bqhc,bkc->bhqk', q_lat, ckv_new) + jnp.einsum('bqhr,bkr->bhqk', q_rope, krope_new)
    causal = jnp.arange(T)[None, :] <= jnp.arange(T)[:, None]
    s_new = jnp.where(causal, s_new.astype(jnp.float32), NEG_INF)
    s = jnp.concatenate([s_past.astype(jnp.float32), s_new], axis=-1) * scale
    prob = jax.nn.softmax(s, axis=-1).astype(ckv_new.dtype)
    return (jnp.einsum('bhqk,bkc->bqhc', prob[..., :past], ckv_past)
            + jnp.einsum('bhqk,bkc->bqhc', prob[..., past:], ckv_new))


def conformer_conv(conv_full, p):
    h = lax.conv_general_dilated(conv_full, p['conv_w'][:, None, :], window_strides=(1,), padding='VALID',
                                 dimension_numbers=('NWC', 'WIO', 'NWC'),
                                 feature_group_count=CONV_CH) + p['conv_b']
    h = jax.nn.silu(layer_norm(h, p['conv_ln_g'], p['conv_ln_b']))
    return jnp.einsum('btc,cd->btd', h, p['w_pw2']), conv_full[:, -(CONV_K - 1):]


def mem_attention(x, mem_k, mem_v, wq, wo):
    q = jnp.einsum('btd,dhe->bthe', x, wq)
    s = jnp.einsum('bthe,bmhe->bhtm', q, mem_k).astype(jnp.float32) * (XA_HEAD_DIM ** -0.5)
    prob = jax.nn.softmax(s, axis=-1).astype(x.dtype)
    o = jnp.einsum('bhtm,bmhe->bthe', prob, mem_v)
    return jnp.einsum('bthe,hed->btd', o, wo)


def hier_moe(x, p):
    B, T, D = x.shape
    xt = x.reshape(B * T, D)
    g_logits = (xt @ p['router_group_w'] + p['router_group_b']).astype(jnp.float32)
    g_prob = jax.nn.softmax(g_logits, axis=-1)
    g_sel = jnp.argmax(g_logits, axis=-1)
    g_w = jnp.max(g_prob, axis=-1, keepdims=True)
    e_logits = (xt @ p['router_expert_w'] + p['router_expert_b']).astype(jnp.float32)
    e_logits = e_logits.reshape(B * T, N_GROUPS, EXP_PER_GROUP)
    e_in = jnp.einsum('ng,nge->ne', jax.nn.one_hot(g_sel, N_GROUPS, dtype=jnp.float32), e_logits)
    top_v, top_i = lax.top_k(e_in, TOP_K_IN_GROUP)
    w2 = jax.nn.softmax(top_v, axis=-1) * g_w
    expert_id = g_sel[:, None] * EXP_PER_GROUP + top_i
    combine = jnp.einsum('nk,nke->ne', w2, jax.nn.one_hot(expert_id, N_EXPERTS, dtype=jnp.float32))
    h = jax.nn.silu(jnp.einsum('nd,edf->nef', xt, p['w_gate_e'])) * jnp.einsum('nd,edf->nef', xt, p['w_up_e'])
    h = h * combine[..., None].astype(h.dtype)
    return jnp.einsum('nef,efd->nd', h, p['w_down_e']).reshape(B, T, D)


def decoder_layer(x, pos, conv_buf, mem_k, mem_v, ckv_past, krope_past, p):
    q_lat, q_rope, ckv, krope, glu, gates = mixer_projections(x, pos, p)
    if ckv_past is None:
        lat = mla_prompt(q_lat, q_rope, ckv, krope)
    else:
        lat = mla_sample(q_lat, q_rope, ckv, krope, ckv_past, krope_past)
    v = jnp.einsum('bthc,chv->bthv', lat, p['w_uv'])
    branch_a = jnp.einsum('bthv,hvd->btd', v, p['w_o_mla'])
    branch_b, new_conv = conformer_conv(jnp.concatenate([conv_buf, glu], axis=1), p)
    merged = gates[..., :D_MODEL] * branch_a + gates[..., D_MODEL:] * branch_b
    mix = jnp.einsum('btd,de->bte', merged, p['w_out'])
    x = layer_norm(DEEPNORM_ALPHA * x + mix, p['ln_g'][0], p['ln_b'][0])
    x = layer_norm(DEEPNORM_ALPHA * x + mem_attention(x, mem_k, mem_v, p['xa_wq'], p['xa_wo']),
                   p['ln_g'][1], p['ln_b'][1])
    x = layer_norm(DEEPNORM_ALPHA * x + hier_moe(x, p), p['ln_g'][2], p['ln_b'][2])
    return x, ckv, krope, new_conv


def setup_inputs(seed: int = 0) -> dict:
    key = jax.random.key(seed)
    ks = iter(jax.random.split(key, 48))

    def nrm(shape, scale):
        return jax.random.normal(next(ks), shape, jnp.float32) * scale

    n_pages = PAST_LEN // PAGE_SIZE
    n_used = DEC_BATCH * n_pages
    n_pool = -(-n_used * 5 // 4)
    page_table = jax.random.permutation(next(ks), n_pool)[:n_used].reshape(DEC_BATCH, n_pages).astype(jnp.int32)
    beta = DEEPNORM_BETA
    return {
        'x_prompt': nrm((BATCH, SEQ, D_MODEL), 1.0),
        'x_sample': nrm((DEC_BATCH, DEC_SEQ, D_MODEL), 1.0),
        'mem_prompt': nrm((BATCH, N_MEM, D_MODEL), 1.0),
        'cache_ckv': nrm((DEPTH, n_pool, PAGE_SIZE, KV_LORA), 1.0),
        'cache_krope': nrm((DEPTH, n_pool, PAGE_SIZE, QK_ROPE), 1.0),
        'cache_mem_k': nrm((DEPTH, DEC_BATCH, N_MEM, XA_HEADS, XA_HEAD_DIM), 1.0),
        'cache_mem_v': nrm((DEPTH, DEC_BATCH, N_MEM, XA_HEADS, XA_HEAD_DIM), beta),
        'state_conv': nrm((DEPTH, DEC_BATCH, CONV_K - 1, CONV_CH), 0.5),
        'page_table': page_table,
        'w_in': nrm((DEPTH, D_MODEL, P_IN), D_MODEL ** -0.5),
        'b_gate': nrm((DEPTH, 2 * D_MODEL), 0.1),
        'q_norm_g': 1.0 + nrm((DEPTH, Q_LORA), 0.01),
        'w_uq': nrm((DEPTH, Q_LORA, N_HEADS, QK_NOPE + QK_ROPE), Q_LORA ** -0.5),
        'kv_norm_g': 1.0 + nrm((DEPTH, KV_LORA), 0.01),
        'w_uk': nrm((DEPTH, KV_LORA, N_HEADS, QK_NOPE), KV_LORA ** -0.5),
        'w_uv': nrm((DEPTH, KV_LORA, N_HEADS, V_HEAD), KV_LORA ** -0.5 * beta),
        'w_o_mla': nrm((DEPTH, N_HEADS, V_HEAD, D_MODEL), (N_HEADS * V_HEAD) ** -0.5),
        'conv_w': nrm((DEPTH, CONV_K, CONV_CH), CONV_K ** -0.5),
        'conv_b': nrm((DEPTH, CONV_CH), 0.01),
        'conv_ln_g': 1.0 + nrm((DEPTH, CONV_CH), 0.01),
        'conv_ln_b': nrm((DEPTH, CONV_CH), 0.01),
        'w_pw2': nrm((DEPTH, CONV_CH, D_MODEL), CONV_CH ** -0.5),
        'w_out': nrm((DEPTH, D_MODEL, D_MODEL), D_MODEL ** -0.5 * beta),
        'xa_wq': nrm((DEPTH, D_MODEL, XA_HEADS, XA_HEAD_DIM), D_MODEL ** -0.5),
        'xa_wk': nrm((DEPTH, D_MODEL, XA_HEADS, XA_HEAD_DIM), D_MODEL ** -0.5),
        'xa_wv': nrm((DEPTH, D_MODEL, XA_HEADS, XA_HEAD_DIM), D_MODEL ** -0.5 * beta),
        'xa_wo': nrm((DEPTH, XA_HEADS, XA_HEAD_DIM, D_MODEL), (XA_HEADS * XA_HEAD_DIM) ** -0.5 * beta),
        'router_group_w': nrm((DEPTH, D_MODEL, N_GROUPS), D_MODEL ** -0.5),
        'router_group_b': nrm((DEPTH, N_GROUPS), 0.01),
        'router_expert_w': nrm((DEPTH, D_MODEL, N_EXPERTS), D_MODEL ** -0.5),
        'router_expert_b': nrm((DEPTH, N_EXPERTS), 0.01),
        'w_gate_e': nrm((DEPTH, N_EXPERTS, D_MODEL, D_EXPERT), D_MODEL ** -0.5),
        'w_up_e': nrm((DEPTH, N_EXPERTS, D_MODEL, D_EXPERT), D_MODEL ** -0.5),
        'w_down_e': nrm((DEPTH, N_EXPERTS, D_EXPERT, D_MODEL), D_EXPERT ** -0.5 * beta),
        'ln_g': 1.0 + nrm((DEPTH, 3, D_MODEL), 0.01),
        'ln_b': nrm((DEPTH, 3, D_MODEL), 0.01),
    }


def reference(x_prompt, x_sample, mem_prompt, cache_ckv, cache_krope, cache_mem_k, cache_mem_v, state_conv,
              page_table, w_in, b_gate, q_norm_g, w_uq, kv_norm_g, w_uk, w_uv, w_o_mla, conv_w, conv_b,
              conv_ln_g, conv_ln_b, w_pw2, w_out, xa_wq, xa_wk, xa_wv, xa_wo, router_group_w, router_group_b,
              router_expert_w, router_expert_b, w_gate_e, w_up_e, w_down_e, ln_g, ln_b):
    B, S = x_prompt.shape[:2]
    DB, T = x_sample.shape[:2]
    n_pages = page_table.shape[1]
    past_len = n_pages * PAGE_SIZE
    pos_prompt = jnp.arange(S, dtype=jnp.int32)
    pos_sample = past_len + jnp.arange(T, dtype=jnp.int32)
    xp, xs = x_prompt, x_sample
    ckv_p, kr_p, mk_p, mv_p, cv_p, ckv_s, kr_s, cv_s = [], [], [], [], [], [], [], []
    for l in range(DEPTH):
        p = dict(w_in=w_in[l], b_gate=b_gate[l], q_norm_g=q_norm_g[l], w_uq=w_uq[l], kv_norm_g=kv_norm_g[l],
                 w_uk=w_uk[l], w_uv=w_uv[l], w_o_mla=w_o_mla[l], conv_w=conv_w[l], conv_b=conv_b[l],
                 conv_ln_g=conv_ln_g[l], conv_ln_b=conv_ln_b[l], w_pw2=w_pw2[l], w_out=w_out[l],
                 xa_wq=xa_wq[l], xa_wo=xa_wo[l], router_group_w=router_group_w[l],
                 router_group_b=router_group_b[l], router_expert_w=router_expert_w[l],
                 router_expert_b=router_expert_b[l], w_gate_e=w_gate_e[l], w_up_e=w_up_e[l],
                 w_down_e=w_down_e[l], ln_g=ln_g[l], ln_b=ln_b[l])
        mem_k = jnp.einsum('bmd,dhe->bmhe', mem_prompt, xa_wk[l])
        mem_v = jnp.einsum('bmd,dhe->bmhe', mem_prompt, xa_wv[l])
        conv0 = jnp.zeros((B, CONV_K - 1, CONV_CH), xp.dtype)
        xp, ckv_l, kr_l, cv_l = decoder_layer(xp, pos_prompt, conv0, mem_k, mem_v, None, None, p)
        ckv_p.append(ckv_l); kr_p.append(kr_l); mk_p.append(mem_k); mv_p.append(mem_v); cv_p.append(cv_l)
        ckv_past = cache_ckv[l][page_table].reshape(DB, past_len, KV_LORA)
        kr_past = cache_krope[l][page_table].reshape(DB, past_len, QK_ROPE)
        xs, ckv_l, kr_l, cv_l = decoder_layer(xs, pos_sample, state_conv[l], cache_mem_k[l], cache_mem_v[l],
                                              ckv_past, kr_past, p)
        ckv_s.append(ckv_l); kr_s.append(kr_l); cv_s.append(cv_l)
    return (xp, xs, jnp.stack(ckv_p), jnp.stack(kr_p), jnp.stack(mk_p), jnp.stack(mv_p), jnp.stack(cv_p),
            jnp.stack(ckv_s), jnp.stack(kr_s), jnp.stack(cv_s))
```

```python
import functools

import jax
import jax.numpy as jnp
from jax import lax
from jax.experimental import pallas as pl
from jax.experimental.pallas import tpu as pltpu

D_MODEL = 2048
BATCH = 4
SEQ = 2048
DEPTH = 2
DEC_BATCH = 128
DEC_SEQ = 8
PAGE_SIZE = 128
N_HEADS = 16
QK_NOPE = 128
QK_ROPE = 64
V_HEAD = 128
Q_LORA = 512
KV_LORA = 512
ROPE_THETA = 10000.0
CONV_CH = D_MODEL
CONV_K = 31
N_MEM = 256
XA_HEADS = 4
XA_HEAD_DIM = 128
N_GROUPS = 4
EXP_PER_GROUP = 4
N_EXPERTS = 16
D_EXPERT = 512
LN_EPS = 1e-5
RMS_EPS = 1e-6
DEEPNORM_ALPHA = (2 * DEPTH) ** 0.25
NEG_INF = -1e30

P_TOK = BATCH * SEQ
S_TOK = DEC_BATCH * DEC_SEQ
N_TOK = P_TOK + S_TOK
C1 = Q_LORA + KV_LORA + QK_ROPE
MLA_SCALE = (QK_NOPE + QK_ROPE) ** -0.5
XA_SCALE = XA_HEAD_DIM ** -0.5
HALO = 32
ROUTER_LANES = 128
VMEM_LIMIT = 56 * 1024 * 1024

BF16 = jnp.bfloat16
F32 = jnp.float32


def _cparams(*sem):
    return pltpu.CompilerParams(dimension_semantics=sem, vmem_limit_bytes=VMEM_LIMIT)


def _dot(a, b):
    return jnp.dot(a, b, preferred_element_type=F32)


def _dot_nt(a, b):
    return lax.dot_general(a, b, (((1,), (1,)), ((), ())), preferred_element_type=F32)


def _layer_norm(z, g, b):
    mu = jnp.mean(z, axis=-1, keepdims=True)
    zc = z - mu
    var = jnp.mean(zc * zc, axis=-1, keepdims=True)
    return zc * lax.rsqrt(var + LN_EPS) * g + b


def _sigmoid(x):
    return 1.0 / (1.0 + jnp.exp(-x))


def _mm_body(a_ref, b_ref, o_ref):
    o_ref[...] = _dot(a_ref[...], b_ref[...])


def _mm(a, b, tm, tn):
    m, k = a.shape
    n = b.shape[1]
    return pl.pallas_call(
        _mm_body,
        grid=(m // tm, n // tn),
        in_specs=[pl.BlockSpec((tm, k), lambda i, j: (i, 0)),
                  pl.BlockSpec((k, tn), lambda i, j: (0, j))],
        out_specs=pl.BlockSpec((tm, tn), lambda i, j: (i, j)),
        out_shape=jax.ShapeDtypeStruct((m, n), F32),
        compiler_params=_cparams("parallel", "parallel"),
        name="mm",
    )(a, b)


def _proj1_body(x_ref, w_ref, qg_ref, kg_ref, cos_ref, sin_ref,
                qn_ref, ckv_ref, kr_ref, ckvb_ref, krb_ref):
    acc = _dot(x_ref[...], w_ref[...])
    qc = acc[:, :Q_LORA]
    kvc = acc[:, Q_LORA:Q_LORA + KV_LORA]
    kr = acc[:, Q_LORA + KV_LORA:]
    qn = qc * lax.rsqrt(jnp.mean(qc * qc, axis=-1, keepdims=True) + RMS_EPS) * qg_ref[...]
    qn_ref[...] = qn.astype(BF16)
    ckv = kvc * lax.rsqrt(jnp.mean(kvc * kvc, axis=-1, keepdims=True) + RMS_EPS) * kg_ref[...]
    ckv_ref[...] = ckv
    ckvb_ref[...] = ckv.astype(BF16)
    half = QK_ROPE // 2
    swapped = jnp.concatenate([kr[:, half:], kr[:, :half]], axis=-1)
    rot = kr * cos_ref[...] + swapped * sin_ref[...]
    kr_ref[...] = rot
    krb_ref[...] = rot.astype(BF16)


def _pos_block_map(tm):
    n_p = P_TOK // tm
    per_seq = SEQ // tm
    return lambda i: (jnp.where(i < n_p, i % per_seq, per_seq), 0)


def _proj1(xb, w1, qg, kg, cos64, sin64, tm):
    row = lambda i: (i, 0)
    const = lambda i: (0, 0)
    return pl.pallas_call(
        _proj1_body,
        grid=(N_TOK // tm,),
        in_specs=[pl.BlockSpec((tm, D_MODEL), row),
                  pl.BlockSpec((D_MODEL, C1), const),
                  pl.BlockSpec((1, Q_LORA), const),
                  pl.BlockSpec((1, KV_LORA), const),
                  pl.BlockSpec((tm, QK_ROPE), _pos_block_map(tm)),
                  pl.BlockSpec((tm, QK_ROPE), _pos_block_map(tm))],
        out_specs=[pl.BlockSpec((tm, Q_LORA), row),
                   pl.BlockSpec((tm, KV_LORA), row),
                   pl.BlockSpec((tm, QK_ROPE), row),
                   pl.BlockSpec((tm, KV_LORA), row),
                   pl.BlockSpec((tm, QK_ROPE), row)],
        out_shape=[jax.ShapeDtypeStruct((N_TOK, Q_LORA), BF16),
                   jax.ShapeDtypeStruct((N_TOK, KV_LORA), F32),
                   jax.ShapeDtypeStruct((N_TOK, QK_ROPE), F32),
                   jax.ShapeDtypeStruct((N_TOK, KV_LORA), BF16),
                   jax.ShapeDtypeStruct((N_TOK, QK_ROPE), BF16)],
        compiler_params=_cparams("parallel"),
        name="proj1",
    )(xb, w1, qg, kg, cos64, sin64)


def _glu_body(x_ref, wa_ref, wb_ref, o_ref):
    x = x_ref[...]
    a = _dot(x, wa_ref[...])
    b = _dot(x, wb_ref[...])
    o_ref[...] = a * _sigmoid(b)


def _glu(xb, wa, wb, tm, tn):
    return pl.pallas_call(
        _glu_body,
        grid=(N_TOK // tm, CONV_CH // tn),
        in_specs=[pl.BlockSpec((tm, D_MODEL), lambda i, j: (i, 0)),
                  pl.BlockSpec((D_MODEL, tn), lambda i, j: (0, j)),
                  pl.BlockSpec((D_MODEL, tn), lambda i, j: (0, j))],
        out_specs=pl.BlockSpec((tm, tn), lambda i, j: (i, j)),
        out_shape=jax.ShapeDtypeStruct((N_TOK, CONV_CH), F32),
        compiler_params=_cparams("parallel", "parallel"),
        name="glu",
    )(xb, wa, wb)


CONV_ROWS = 32
CONV_LANES = 512
LN_ROWS = 16


def _conv_taps(win_ref, win_row0, w_ref, b_ref, h_ref, h_row0, rows):
    def chan_chunk(c, carry):
        c0 = pl.multiple_of(c * CONV_LANES, CONV_LANES)
        cs = pl.ds(c0, CONV_LANES)
        acc = jnp.broadcast_to(b_ref[:, cs], (rows, CONV_LANES))
        for k in range(CONV_K):
            acc = acc + w_ref[k:k + 1, cs] * win_ref[win_row0 + k:win_row0 + k + rows, cs]
        h_ref[h_row0:h_row0 + rows, cs] = acc
        return carry
    lax.fori_loop(0, CONV_CH // CONV_LANES, chan_chunk, 0)


def _ln_swish_rows(h_ref, g_ref, be_ref, o_ref, n_rows):
    def row_chunk(r, carry):
        rs = pl.ds(pl.multiple_of(r * LN_ROWS, LN_ROWS), LN_ROWS)
        y = _layer_norm(h_ref[rs, :], g_ref[...], be_ref[...])
        o_ref[rs, :] = (y * _sigmoid(y)).astype(o_ref.dtype)
        return carry
    lax.fori_loop(0, n_rows // LN_ROWS, row_chunk, 0)


def _conv_p_body(main_ref, prev_ref, w_ref, b_ref, g_ref, be_ref, o_ref, win_sc, h_sc, *, tt):
    first = pl.program_id(1) == 0
    win_sc[0:HALO, :] = jnp.where(first, 0.0, prev_ref[...])
    win_sc[HALO:HALO + tt, :] = main_ref[...]
    lead = HALO - (CONV_K - 1)
    for r0 in range(0, tt, CONV_ROWS):
        _conv_taps(win_sc, r0 + lead, w_ref, b_ref, h_sc, r0, CONV_ROWS)
    _ln_swish_rows(h_sc, g_ref, be_ref, o_ref, tt)


def _conv_prompt(glu, w, b, g, be, tt):
    n_t = SEQ // tt
    const = lambda bi, i: (0, 0)
    return pl.pallas_call(
        functools.partial(_conv_p_body, tt=tt),
        grid=(BATCH, n_t),
        in_specs=[pl.BlockSpec((tt, CONV_CH), lambda bi, i: (bi * n_t + i, 0)),
                  pl.BlockSpec((HALO, CONV_CH),
                               lambda bi, i: (jnp.maximum((bi * SEQ + i * tt) // HALO - 1, 0), 0)),
                  pl.BlockSpec((HALO, CONV_CH), const),
                  pl.BlockSpec((1, CONV_CH), const),
                  pl.BlockSpec((1, CONV_CH), const),
                  pl.BlockSpec((1, CONV_CH), const)],
        out_specs=pl.BlockSpec((tt, CONV_CH), lambda bi, i: (bi * n_t + i, 0)),
        out_shape=jax.ShapeDtypeStruct((P_TOK, CONV_CH), BF16),
        scratch_shapes=[pltpu.VMEM((HALO + tt, CONV_CH), F32), pltpu.VMEM((tt, CONV_CH), F32)],
        compiler_params=_cparams("parallel", "parallel"),
        name="conv_prompt",
    )(glu, glu, w, b, g, be)


def _conv_s_body(state_ref, new_ref, w_ref, b_ref, g_ref, be_ref, o_ref, win_sc, h_sc, *, gb):
    hist = CONV_K - 1
    for gi in range(gb):
        win_sc[0:hist, :] = state_ref[gi]
        win_sc[hist:hist + DEC_SEQ, :] = new_ref[gi]
        _conv_taps(win_sc, 0, w_ref, b_ref, h_sc, gi * DEC_SEQ, DEC_SEQ)
    _ln_swish_rows(h_sc, g_ref, be_ref, o_ref, gb * DEC_SEQ)


def _conv_sample(state, glu3, w, b, g, be, gb):
    const = lambda i: (0, 0)
    return pl.pallas_call(
        functools.partial(_conv_s_body, gb=gb),
        grid=(DEC_BATCH // gb,),
        in_specs=[pl.BlockSpec((gb, CONV_K - 1, CONV_CH), lambda i: (i, 0, 0)),
                  pl.BlockSpec((gb, DEC_SEQ, CONV_CH), lambda i: (i, 0, 0)),
                  pl.BlockSpec((HALO, CONV_CH), const),
                  pl.BlockSpec((1, CONV_CH), const),
                  pl.BlockSpec((1, CONV_CH), const),
                  pl.BlockSpec((1, CONV_CH), const)],
        out_specs=pl.BlockSpec((gb * DEC_SEQ, CONV_CH), lambda i: (i, 0)),
        out_shape=jax.ShapeDtypeStruct((S_TOK, CONV_CH), BF16),
        scratch_shapes=[pltpu.VMEM((CONV_K - 1 + DEC_SEQ + 2, CONV_CH), F32),
                        pltpu.VMEM((gb * DEC_SEQ, CONV_CH), F32)],
        compiler_params=_cparams("parallel"),
        name="conv_sample",
    )(state, glu3, w, b, g, be)


def _q_body(qn_ref, wuq_ref, wuk_ref, cos_ref, sin_ref, ql_ref, qr_ref, *, sample):
    tm = qn_ref.shape[0]
    q = _dot(qn_ref[...], wuq_ref[...])
    nope_w = N_HEADS * QK_NOPE
    rope_w = N_HEADS * QK_ROPE
    qr = q[:, nope_w:]
    half = QK_ROPE // 2
    lane = lax.broadcasted_iota(jnp.int32, qr.shape, 1)
    ahead = pltpu.roll(qr, rope_w - half, 1)
    behind = pltpu.roll(qr, half, 1)
    swapped = jnp.where((lane % QK_ROPE) < half, ahead, behind)
    rot = qr * cos_ref[...] + swapped * sin_ref[...]
    for h in range(N_HEADS):
        nope = q[:, h * QK_NOPE:(h + 1) * QK_NOPE].astype(BF16)
        lat = _dot(nope, wuk_ref[h])
        rot_h = rot[:, h * QK_ROPE:(h + 1) * QK_ROPE]
        if sample:
            ql_ref[:, h, :, :] = lat.reshape(tm // DEC_SEQ, DEC_SEQ, KV_LORA)
            qr_ref[:, h, :, :] = rot_h.reshape(tm // DEC_SEQ, DEC_SEQ, QK_ROPE)
        else:
            ql_ref[0, h] = lat.astype(BF16)
            qr_ref[0, h] = rot_h.astype(BF16)


def _q_prompt(qn, wuq, wuk, cos_q, sin_q, tm):
    n_t = SEQ // tm
    c2 = lambda b, i: (0, 0)
    c3 = lambda b, i: (0, 0, 0)
    return pl.pallas_call(
        functools.partial(_q_body, sample=False),
        grid=(BATCH, n_t),
        in_specs=[pl.BlockSpec((tm, Q_LORA), lambda b, i: (b * n_t + i, 0)),
                  pl.BlockSpec((Q_LORA, N_HEADS * (QK_NOPE + QK_ROPE)), c2),
                  pl.BlockSpec((N_HEADS, QK_NOPE, KV_LORA), c3),
                  pl.BlockSpec((tm, N_HEADS * QK_ROPE), lambda b, i: (i, 0)),
                  pl.BlockSpec((tm, N_HEADS * QK_ROPE), lambda b, i: (i, 0))],
        out_specs=[pl.BlockSpec((1, N_HEADS, tm, KV_LORA), lambda b, i: (b, 0, i, 0)),
                   pl.BlockSpec((1, N_HEADS, tm, QK_ROPE), lambda b, i: (b, 0, i, 0))],
        out_shape=[jax.ShapeDtypeStruct((BATCH, N_HEADS, SEQ, KV_LORA), BF16),
                   jax.ShapeDtypeStruct((BATCH, N_HEADS, SEQ, QK_ROPE), BF16)],
        compiler_params=_cparams("parallel", "parallel"),
        name="q_prompt",
    )(qn, wuq, wuk, cos_q, sin_q)


def _q_sample(qn, wuq, wuk, cos_q, sin_q, tm):
    gb = tm // DEC_SEQ
    pos_blk = SEQ // tm
    c2 = lambda i: (0, 0)
    c3 = lambda i: (0, 0, 0)
    return pl.pallas_call(
        functools.partial(_q_body, sample=True),
        grid=(S_TOK // tm,),
        in_specs=[pl.BlockSpec((tm, Q_LORA), lambda i: (P_TOK // tm + i, 0)),
                  pl.BlockSpec((Q_LORA, N_HEADS * (QK_NOPE + QK_ROPE)), c2),
                  pl.BlockSpec((N_HEADS, QK_NOPE, KV_LORA), c3),
                  pl.BlockSpec((tm, N_HEADS * QK_ROPE), lambda i: (pos_blk, 0)),
                  pl.BlockSpec((tm, N_HEADS * QK_ROPE), lambda i: (pos_blk, 0))],
        out_specs=[pl.BlockSpec((gb, N_HEADS, DEC_SEQ, KV_LORA), lambda i: (i, 0, 0, 0)),
                   pl.BlockSpec((gb, N_HEADS, DEC_SEQ, QK_ROPE), lambda i: (i, 0, 0, 0))],
        out_shape=[jax.ShapeDtypeStruct((DEC_BATCH, N_HEADS, DEC_SEQ, KV_LORA), F32),
                   jax.ShapeDtypeStruct((DEC_BATCH, N_HEADS, DEC_SEQ, QK_ROPE), F32)],
        compiler_params=_cparams("parallel"),
        name="q_sample",
    )(qn, wuq, wuk, cos_q, sin_q)


def _attn_p_body(ql_ref, qr_ref, k_ref, kr_ref, wuv_ref, o_ref, m_sc, l_sc, acc_sc, *, tq, tk):
    qi = pl.program_id(1)
    rows = N_HEADS * tq
    ql = ql_ref[0].reshape(rows, KV_LORA)
    qr = qr_ref[0].reshape(rows, QK_ROPE)
    m_sc[...] = jnp.full(m_sc.shape, NEG_INF, F32)
    l_sc[...] = jnp.zeros(l_sc.shape, F32)
    acc_sc[...] = jnp.zeros(acc_sc.shape, F32)
    n_chunks = ((qi + 1) * tq + tk - 1) // tk
    qpos = qi * tq + lax.broadcasted_iota(jnp.int32, (tq, tk), 0)
    kidx = lax.broadcasted_iota(jnp.int32, (tq, tk), 1)

    def chunk(j, carry):
        ks = pl.ds(pl.multiple_of(j * tk, tk), tk)
        k = k_ref[0, ks, :]
        kr = kr_ref[0, ks, :]
        s = (_dot_nt(ql, k) + _dot_nt(qr, kr)) * MLA_SCALE
        visible = (j * tk + kidx) <= qpos
        s = jnp.where(visible[None], s.reshape(N_HEADS, tq, tk), NEG_INF).reshape(rows, tk)
        m_prev = m_sc[...]
        m_new = jnp.maximum(m_prev, jnp.max(s, axis=-1, keepdims=True))
        alpha = jnp.exp(m_prev - m_new)
        p = jnp.exp(s - m_new)
        l_sc[...] = alpha * l_sc[...] + jnp.sum(p, axis=-1, keepdims=True)
        acc_sc[...] = alpha * acc_sc[...] + _dot(p.astype(BF16), k)
        m_sc[...] = m_new
        return carry

    lax.fori_loop(0, n_chunks, chunk, 0)
    lat = acc_sc[...] / l_sc[...]
    for h in range(N_HEADS):
        lat_h = lat[h * tq:(h + 1) * tq, :].astype(BF16)
        o_ref[:, h * V_HEAD:(h + 1) * V_HEAD] = _dot(lat_h, wuv_ref[h]).astype(BF16)


def _attn_prompt(ql, qr, ckvb, krb, wuv, tq, tk):
    n_q = SEQ // tq
    rows = N_HEADS * tq
    return pl.pallas_call(
        functools.partial(_attn_p_body, tq=tq, tk=tk),
        grid=(BATCH, n_q),
        in_specs=[pl.BlockSpec((1, N_HEADS, tq, KV_LORA), lambda b, i: (b, 0, i, 0)),
                  pl.BlockSpec((1, N_HEADS, tq, QK_ROPE), lambda b, i: (b, 0, i, 0)),
                  pl.BlockSpec((1, SEQ, KV_LORA), lambda b, i: (b, 0, 0)),
                  pl.BlockSpec((1, SEQ, QK_ROPE), lambda b, i: (b, 0, 0)),
                  pl.BlockSpec((N_HEADS, KV_LORA, V_HEAD), lambda b, i: (0, 0, 0))],
        out_specs=pl.BlockSpec((tq, N_HEADS * V_HEAD), lambda b, i: (b * n_q + i, 0)),
        out_shape=jax.ShapeDtypeStruct((P_TOK, N_HEADS * V_HEAD), BF16),
        scratch_shapes=[pltpu.VMEM((rows, 1), F32), pltpu.VMEM((rows, 1), F32),
                        pltpu.VMEM((rows, KV_LORA), F32)],
        compiler_params=_cparams("parallel", "arbitrary"),
        name="attn_prompt",
    )(ql, qr, ckvb, krb, wuv)


def _attn_s_body(pt_ref, ql_ref, qr_ref, cn_ref, kn_ref, *refs, pages):
    ck_refs = refs[:pages]
    kr_refs = refs[pages:2 * pages]
    o_ref = refs[2 * pages]
    m_sc, l_sc, acc_sc, kb_sc, krb_sc = refs[2 * pages + 1:]
    g = pl.program_id(1)
    rows = N_HEADS * DEC_SEQ
    ql = ql_ref[0].reshape(rows, KV_LORA)
    qr = qr_ref[0].reshape(rows, QK_ROPE)

    @pl.when(g == 0)
    def _():
        cn = cn_ref[0]
        kn = kn_ref[0]
        s = (_dot_nt(ql, cn) + _dot_nt(qr, kn)) * MLA_SCALE
        t = lax.broadcasted_iota(jnp.int32, s.shape, 0) % DEC_SEQ
        c = lax.broadcasted_iota(jnp.int32, s.shape, 1)
        s = jnp.where(c <= t, s, NEG_INF)
        m = jnp.max(s, axis=-1, keepdims=True)
        p = jnp.exp(s - m)
        m_sc[...] = m
        l_sc[...] = jnp.sum(p, axis=-1, keepdims=True)
        acc_sc[...] = _dot(p, cn)

    for u in range(pages):
        kb_sc[u * PAGE_SIZE:(u + 1) * PAGE_SIZE, :] = ck_refs[u][...].astype(BF16)
        krb_sc[u * PAGE_SIZE:(u + 1) * PAGE_SIZE, :] = kr_refs[u][...].astype(BF16)
    kb = kb_sc[...]
    s = (_dot_nt(ql.astype(BF16), kb) + _dot_nt(qr.astype(BF16), krb_sc[...])) * MLA_SCALE
    m_prev = m_sc[...]
    m_new = jnp.maximum(m_prev, jnp.max(s, axis=-1, keepdims=True))
    alpha = jnp.exp(m_prev - m_new)
    p = jnp.exp(s - m_new)
    l_sc[...] = alpha * l_sc[...] + jnp.sum(p, axis=-1, keepdims=True)
    acc_sc[...] = alpha * acc_sc[...] + _dot(p.astype(BF16), kb)
    m_sc[...] = m_new

    @pl.when(g == pl.num_programs(1) - 1)
    def _():
        o_ref[0] = (acc_sc[...] / l_sc[...]).reshape(N_HEADS, DEC_SEQ, KV_LORA)


def _attn_sample(page_table, ql, qr, ckv_new, kr_new, cache_ckv, cache_krope, layer, pages):
    n_pages = page_table.shape[1]
    n_g = n_pages // pages
    rows = N_HEADS * DEC_SEQ

    def page_spec(width, u):
        return pl.BlockSpec((None, None, PAGE_SIZE, width),
                            lambda b, g, pt: (layer, pt[b * n_pages + g * pages + u], 0, 0))

    per_b4 = lambda b, g, pt: (b, 0, 0, 0)
    per_b3 = lambda b, g, pt: (b, 0, 0)
    grid_spec = pltpu.PrefetchScalarGridSpec(
        num_scalar_prefetch=1,
        grid=(DEC_BATCH, n_g),
        in_specs=[pl.BlockSpec((1, N_HEADS, DEC_SEQ, KV_LORA), per_b4),
                  pl.BlockSpec((1, N_HEADS, DEC_SEQ, QK_ROPE), per_b4),
                  pl.BlockSpec((1, DEC_SEQ, KV_LORA), per_b3),
                  pl.BlockSpec((1, DEC_SEQ, QK_ROPE), per_b3)]
                 + [page_spec(KV_LORA, u) for u in range(pages)]
                 + [page_spec(QK_ROPE, u) for u in range(pages)],
        out_specs=pl.BlockSpec((1, N_HEADS, DEC_SEQ, KV_LORA), per_b4),
        scratch_shapes=[pltpu.VMEM((rows, 1), F32), pltpu.VMEM((rows, 1), F32),
                        pltpu.VMEM((rows, KV_LORA), F32),
                        pltpu.VMEM((pages * PAGE_SIZE, KV_LORA), BF16),
                        pltpu.VMEM((pages * PAGE_SIZE, QK_ROPE), BF16)])
    return pl.pallas_call(
        functools.partial(_attn_s_body, pages=pages),
        grid_spec=grid_spec,
        out_shape=jax.ShapeDtypeStruct((DEC_BATCH, N_HEADS, DEC_SEQ, KV_LORA), F32),
        compiler_params=_cparams("parallel", "arbitrary"),
        name="attn_sample",
    )(page_table.reshape(-1), ql, qr, ckv_new, kr_new,
      *([cache_ckv] * pages), *([cache_krope] * pages))


def _vup_s_body(lat_ref, wuv_ref, o_ref):
    lat = lat_ref[...].reshape(S_TOK, KV_LORA).astype(BF16)
    o_ref[...] = _dot(lat, wuv_ref[0]).astype(BF16)


def _vup_sample(lat, wuv):
    return pl.pallas_call(
        _vup_s_body,
        grid=(N_HEADS,),
        in_specs=[pl.BlockSpec((DEC_BATCH, None, DEC_SEQ, KV_LORA), lambda h: (0, h, 0, 0)),
                  pl.BlockSpec((1, KV_LORA, V_HEAD), lambda h: (h, 0, 0))],
        out_specs=pl.BlockSpec((S_TOK, V_HEAD), lambda h: (0, h)),
        out_shape=jax.ShapeDtypeStruct((S_TOK, N_HEADS * V_HEAD), BF16),
        compiler_params=_cparams("parallel"),
        name="vup_sample",
    )(lat, wuv)


def _merge_body(v_ref, h_ref, x_ref, wo_ref, wp_ref, wga_ref, wgb_ref, ba_ref, bb_ref, o_ref):
    x = x_ref[...]
    branch_a = _dot(v_ref[...], wo_ref[...])
    branch_b = _dot(h_ref[...], wp_ref[...])
    gate_a = _sigmoid(_dot(x, wga_ref[...]) + ba_ref[...])
    gate_b = _sigmoid(_dot(x, wgb_ref[...]) + bb_ref[...])
    o_ref[...] = (gate_a * branch_a + gate_b * branch_b).astype(BF16)


def _merge(v, hb, xb, wo, wp, wg, bg, tm, tn):
    n_j = D_MODEL // tn
    row = lambda i, j: (i, 0)
    col = lambda i, j: (0, j)
    col_b = lambda i, j: (0, j + n_j)
    return pl.pallas_call(
        _merge_body,
        grid=(N_TOK // tm, n_j),
        in_specs=[pl.BlockSpec((tm, D_MODEL), row),
                  pl.BlockSpec((tm, D_MODEL), row),
                  pl.BlockSpec((tm, D_MODEL), row),
                  pl.BlockSpec((D_MODEL, tn), col),
                  pl.BlockSpec((D_MODEL, tn), col),
                  pl.BlockSpec((D_MODEL, tn), col),
                  pl.BlockSpec((D_MODEL, tn), col_b),
                  pl.BlockSpec((1, tn), col),
                  pl.BlockSpec((1, tn), col_b)],
        out_specs=pl.BlockSpec((tm, tn), lambda i, j: (i, j)),
        out_shape=jax.ShapeDtypeStruct((N_TOK, D_MODEL), BF16),
        compiler_params=_cparams("parallel", "parallel"),
        name="merge",
    )(v, hb, xb, wo, wp, wg, wg, bg, bg)


def _mm_res_ln_body(a_ref, w_ref, res_ref, g_ref, b_ref, o_ref, ob_ref):
    z = DEEPNORM_ALPHA * res_ref[...] + _dot(a_ref[...], w_ref[...])
    y = _layer_norm(z, g_ref[...], b_ref[...])
    o_ref[...] = y
    ob_ref[...] = y.astype(BF16)


def _mm_res_ln(a, w, res, g, b, tm):
    k = a.shape[1]
    row = lambda i: (i, 0)
    const = lambda i: (0, 0)
    return pl.pallas_call(
        _mm_res_ln_body,
        grid=(N_TOK // tm,),
        in_specs=[pl.BlockSpec((tm, k), row),
                  pl.BlockSpec((k, D_MODEL), const),
                  pl.BlockSpec((tm, D_MODEL), row),
                  pl.BlockSpec((1, D_MODEL), const),
                  pl.BlockSpec((1, D_MODEL), const)],
        out_specs=[pl.BlockSpec((tm, D_MODEL), row), pl.BlockSpec((tm, D_MODEL), row)],
        out_shape=[jax.ShapeDtypeStruct((N_TOK, D_MODEL), F32),
                   jax.ShapeDtypeStruct((N_TOK, D_MODEL), BF16)],
        compiler_params=_cparams("parallel"),
        name="mm_res_ln",
    )(a, w, res, g, b)


def _xattn_body(x_ref, xb_ref, wq_ref, mk_ref, mv_ref, wo_ref, g_ref, b_ref, o_ref, ob_ref, *, gb, t, op_dtype):
    q = _dot(xb_ref[...], wq_ref[...])
    heads = []
    for h in range(XA_HEADS):
        hs = slice(h * XA_HEAD_DIM, (h + 1) * XA_HEAD_DIM)
        qh = q[:, hs].reshape(gb, t, XA_HEAD_DIM).astype(op_dtype)
        kh = mk_ref[:, :, hs].astype(op_dtype)
        vh = mv_ref[:, :, hs].astype(op_dtype)
        s = jnp.einsum('gte,gme->gtm', qh, kh, preferred_element_type=F32) * XA_SCALE
        e = jnp.exp(s - jnp.max(s, axis=-1, keepdims=True))
        p = e / jnp.sum(e, axis=-1, keepdims=True)
        oh = jnp.einsum('gtm,gme->gte', p.astype(op_dtype), vh, preferred_element_type=F32)
        heads.append(oh.reshape(gb * t, XA_HEAD_DIM))
    o = jnp.concatenate(heads, axis=-1).astype(BF16)
    z = DEEPNORM_ALPHA * x_ref[...] + _dot(o, wo_ref[...])
    y = _layer_norm(z, g_ref[...], b_ref[...])
    o_ref[...] = y
    ob_ref[...] = y.astype(BF16)


def _xattn(x, xb, wq, mk, mv, wo, g, b, *, row0, n_rows, gb, t, op_dtype, name):
    tm = gb * t
    blk0 = row0 // tm
    hd = XA_HEADS * XA_HEAD_DIM
    row = lambda i: (blk0 + i, 0)
    out_row = lambda i: (i, 0)
    const = lambda i: (0, 0)
    groups_per_mem = (n_rows // tm) // (mk.shape[0] // gb)
    mem = lambda i: (i // groups_per_mem, 0, 0)
    return pl.pallas_call(
        functools.partial(_xattn_body, gb=gb, t=t, op_dtype=op_dtype),
        grid=(n_rows // tm,),
        in_specs=[pl.BlockSpec((tm, D_MODEL), row),
                  pl.BlockSpec((tm, D_MODEL), row),
                  pl.BlockSpec((D_MODEL, hd), const),
                  pl.BlockSpec((gb, N_MEM, hd), mem),
                  pl.BlockSpec((gb, N_MEM, hd), mem),
                  pl.BlockSpec((hd, D_MODEL), const),
                  pl.BlockSpec((1, D_MODEL), const),
                  pl.BlockSpec((1, D_MODEL), const)],
        out_specs=[pl.BlockSpec((tm, D_MODEL), out_row), pl.BlockSpec((tm, D_MODEL), out_row)],
        out_shape=[jax.ShapeDtypeStruct((n_rows, D_MODEL), F32),
                   jax.ShapeDtypeStruct((n_rows, D_MODEL), BF16)],
        compiler_params=_cparams("parallel"),
        name=name,
    )(x, xb, wq, mk, mv, wo, g, b)


def _route(logits):
    lane = lax.broadcasted_iota(jnp.int32, logits.shape, 1)
    big = jnp.int32(ROUTER_LANES)
    gl = jnp.where(lane < N_GROUPS, logits, NEG_INF)
    gmax = jnp.max(gl, axis=-1, keepdims=True)
    g_sel = jnp.min(jnp.where(gl == gmax, lane, big), axis=-1, keepdims=True)
    g_w = 1.0 / jnp.sum(jnp.exp(gl - gmax), axis=-1, keepdims=True)
    lo = N_GROUPS + EXP_PER_GROUP * g_sel
    el = jnp.where((lane >= lo) & (lane < lo + EXP_PER_GROUP), logits, NEG_INF)
    v1 = jnp.max(el, axis=-1, keepdims=True)
    i1 = jnp.min(jnp.where(el == v1, lane, big), axis=-1, keepdims=True)
    el2 = jnp.where(lane == i1, NEG_INF, el)
    v2 = jnp.max(el2, axis=-1, keepdims=True)
    i2 = jnp.min(jnp.where(el2 == v2, lane, big), axis=-1, keepdims=True)
    e2 = jnp.exp(v2 - v1)
    w1 = g_w / (1.0 + e2)
    w2 = g_w * e2 / (1.0 + e2)
    return jnp.where(lane == i1, w1, 0.0) + jnp.where(lane == i2, w2, 0.0)


def _moe_body(x_ref, xb_ref, wr_ref, br_ref, wg_ref, wu_ref, wd_ref, g_ref, b_ref,
              o_ref, ob_ref, comb_sc, acc_sc):
    e = pl.program_id(1)

    @pl.when(e == 0)
    def _():
        logits = jnp.dot(x_ref[...], wr_ref[...], preferred_element_type=F32,
                         precision=lax.Precision.HIGHEST) + br_ref[...]
        comb_sc[...] = _route(logits)
        acc_sc[...] = jnp.zeros(acc_sc.shape, F32)

    comb = comb_sc[...]
    lane = lax.broadcasted_iota(jnp.int32, comb.shape, 1)
    c = jnp.sum(jnp.where(lane == N_GROUPS + e, comb, 0.0), axis=-1, keepdims=True)
    xb = xb_ref[...]
    gate = _dot(xb, wg_ref[...])
    up = _dot(xb, wu_ref[...])
    hid = (gate * _sigmoid(gate)) * up * c
    acc_sc[...] += _dot(hid.astype(BF16), wd_ref[...])

    @pl.when(e == pl.num_programs(1) - 1)
    def _():
        y = _layer_norm(DEEPNORM_ALPHA * x_ref[...] + acc_sc[...], g_ref[...], b_ref[...])
        o_ref[...] = y
        ob_ref[...] = y.astype(BF16)


def _moe(x, xb, wr, br, wg, wu, wd, g, b, tm):
    row = lambda i, e: (i, 0)
    const = lambda i, e: (0, 0)
    return pl.pallas_call(
        _moe_body,
        grid=(N_TOK // tm, N_EXPERTS),
        in_specs=[pl.BlockSpec((tm, D_MODEL), row),
                  pl.BlockSpec((tm, D_MODEL), row),
                  pl.BlockSpec((D_MODEL, ROUTER_LANES), const),
                  pl.BlockSpec((1, ROUTER_LANES), const),
                  pl.BlockSpec((None, D_MODEL, D_EXPERT), lambda i, e: (e, 0, 0)),
                  pl.BlockSpec((None, D_MODEL, D_EXPERT), lambda i, e: (e, 0, 0)),
                  pl.BlockSpec((None, D_EXPERT, D_MODEL), lambda i, e: (e, 0, 0)),
                  pl.BlockSpec((1, D_MODEL), const),
                  pl.BlockSpec((1, D_MODEL), const)],
        out_specs=[pl.BlockSpec((tm, D_MODEL), row), pl.BlockSpec((tm, D_MODEL), row)],
        out_shape=[jax.ShapeDtypeStruct((N_TOK, D_MODEL), F32),
                   jax.ShapeDtypeStruct((N_TOK, D_MODEL), BF16)],
        scratch_shapes=[pltpu.VMEM((tm, ROUTER_LANES), F32), pltpu.VMEM((tm, D_MODEL), F32)],
        compiler_params=_cparams("parallel", "arbitrary"),
        name="moe",
    )(x, xb, wr, br, wg, wu, wd, g, b)


def _rotary(pos):
    half = QK_ROPE // 2
    inv = ROPE_THETA ** (-jnp.arange(half, dtype=F32) / half)
    ang = pos.astype(F32)[:, None] * inv[None, :]
    cos, sin = jnp.cos(ang), jnp.sin(ang)
    return jnp.concatenate([cos, cos], axis=-1), jnp.concatenate([-sin, sin], axis=-1)


TM_PROJ = 512
TM_Q = 256
TQ_ATTN = 128
TK_ATTN = 512
TT_CONV = 128
GB_CONV = 8
GB_XA = 8
PAGES_PER_STEP = 16
TM_LN = 256
TM_MOE = 512


def kernel(x_prompt, x_sample, mem_prompt, cache_ckv, cache_krope, cache_mem_k, cache_mem_v, state_conv, page_table, w_in, b_gate, q_norm_g, w_uq, kv_norm_g, w_uk, w_uv, w_o_mla, conv_w, conv_b, conv_ln_g, conv_ln_b, w_pw2, w_out, xa_wq, xa_wk, xa_wv, xa_wo, router_group_w, router_group_b, router_expert_w, router_expert_b, w_gate_e, w_up_e, w_down_e, ln_g, ln_b):
    n_pages = page_table.shape[1]
    past_len = n_pages * PAGE_SIZE
    hd = XA_HEADS * XA_HEAD_DIM

    pos_rows = max(TM_PROJ, TM_Q)
    pos = jnp.concatenate([jnp.arange(SEQ, dtype=jnp.int32),
                           past_len + (jnp.arange(pos_rows, dtype=jnp.int32) % DEC_SEQ)])
    cos64, sin64 = _rotary(pos)
    cos_q = jnp.tile(cos64, (1, N_HEADS))
    sin_q = jnp.tile(sin64, (1, N_HEADS))
    n_pos_k = SEQ + TM_PROJ
    n_pos_q = SEQ + TM_Q

    x = jnp.concatenate([x_prompt.reshape(P_TOK, D_MODEL), x_sample.reshape(S_TOK, D_MODEL)], axis=0)
    xb = x.astype(BF16)

    w_mem = jnp.concatenate([xa_wk.reshape(DEPTH, D_MODEL, hd), xa_wv.reshape(DEPTH, D_MODEL, hd)], axis=-1)
    w_mem = jnp.moveaxis(w_mem, 0, 1).reshape(D_MODEL, DEPTH * 2 * hd).astype(BF16)
    mem_kv = _mm(mem_prompt.reshape(BATCH * N_MEM, D_MODEL).astype(BF16), w_mem, 512, 512)
    mem_kv = mem_kv.reshape(BATCH, N_MEM, DEPTH, 2, hd)

    outs = {k: [] for k in ("ckv_p", "kr_p", "mk_p", "mv_p", "cv_p", "ckv_s", "kr_s", "cv_s")}
    for l in range(DEPTH):
        wl = w_in[l]
        w1 = wl[:, :C1].astype(BF16)
        wa = wl[:, C1:C1 + CONV_CH].astype(BF16)
        wb = wl[:, C1 + CONV_CH:C1 + 2 * CONV_CH].astype(BF16)
        wg = wl[:, C1 + 2 * CONV_CH:].astype(BF16)
        wuq = jnp.concatenate([w_uq[l][:, :, :QK_NOPE].reshape(Q_LORA, N_HEADS * QK_NOPE),
                               w_uq[l][:, :, QK_NOPE:].reshape(Q_LORA, N_HEADS * QK_ROPE)], axis=-1).astype(BF16)
        wuk = jnp.transpose(w_uk[l], (1, 2, 0)).astype(BF16)
        wuv = jnp.transpose(w_uv[l], (1, 0, 2)).astype(BF16)
        wo = w_o_mla[l].reshape(N_HEADS * V_HEAD, D_MODEL).astype(BF16)
        wp = w_pw2[l].astype(BF16)
        wout = w_out[l].astype(BF16)
        wxq = xa_wq[l].reshape(D_MODEL, hd).astype(BF16)
        wxo = xa_wo[l].reshape(hd, D_MODEL).astype(BF16)
        wr = jnp.concatenate([router_group_w[l], router_expert_w[l]], axis=-1)
        wr = jnp.pad(wr, ((0, 0), (0, ROUTER_LANES - wr.shape[1])))
        br = jnp.concatenate([router_group_b[l], router_expert_b[l]])
        br = jnp.pad(br, (0, ROUTER_LANES - br.shape[0]))[None, :]
        conv_w_l = jnp.pad(conv_w[l], ((0, HALO - CONV_K), (0, 0)))
        row = lambda v: v[None, :]

        qn, ckv, kr, ckvb, krb = _proj1(xb, w1, row(q_norm_g[l]), row(kv_norm_g[l]),
                                        cos64[:n_pos_k], sin64[:n_pos_k], TM_PROJ)
        glu = _glu(xb, wa, wb, 512, 512)

        hb_p = _conv_prompt(glu, conv_w_l, row(conv_b[l]), row(conv_ln_g[l]), row(conv_ln_b[l]), TT_CONV)
        glu_s = glu[P_TOK:].reshape(DEC_BATCH, DEC_SEQ, CONV_CH)
        hb_s = _conv_sample(state_conv[l], glu_s, conv_w_l, row(conv_b[l]), row(conv_ln_g[l]),
                            row(conv_ln_b[l]), GB_CONV)
        hb = jnp.concatenate([hb_p, hb_s], axis=0)

        ql_p, qr_p = _q_prompt(qn, wuq, wuk, cos_q[:n_pos_q], sin_q[:n_pos_q], TM_Q)
        ql_s, qr_s = _q_sample(qn, wuq, wuk, cos_q[:n_pos_q], sin_q[:n_pos_q], TM_Q)
        v_p = _attn_prompt(ql_p, qr_p, ckvb[:P_TOK].reshape(BATCH, SEQ, KV_LORA),
                           krb[:P_TOK].reshape(BATCH, SEQ, QK_ROPE), wuv, TQ_ATTN, TK_ATTN)
        ckv_s = ckv[P_TOK:].reshape(DEC_BATCH, DEC_SEQ, KV_LORA)
        kr_s = kr[P_TOK:].reshape(DEC_BATCH, DEC_SEQ, QK_ROPE)
        lat_s = _attn_sample(page_table, ql_s, qr_s, ckv_s, kr_s, cache_ckv, cache_krope, l, PAGES_PER_STEP)
        v_s = _vup_sample(lat_s, wuv)
        v = jnp.concatenate([v_p, v_s], axis=0)

        merged = _merge(v, hb, xb, wo, wp, wg, row(b_gate[l]), 512, 512)
        x1, x1b = _mm_res_ln(merged, wout, x, row(ln_g[l, 0]), row(ln_b[l, 0]), TM_LN)

        mk_p = mem_kv[:, :, l, 0]
        mv_p = mem_kv[:, :, l, 1]
        x2_p, x2b_p = _xattn(x1, x1b, wxq, mk_p, mv_p, wxo, row(ln_g[l, 1]), row(ln_b[l, 1]),
                             row0=0, n_rows=P_TOK, gb=1, t=512, op_dtype=BF16, name="xattn_prompt")
        x2_s, x2b_s = _xattn(x1, x1b, wxq, cache_mem_k[l].reshape(DEC_BATCH, N_MEM, hd),
                             cache_mem_v[l].reshape(DEC_BATCH, N_MEM, hd), wxo,
                             row(ln_g[l, 1]), row(ln_b[l, 1]),
                             row0=P_TOK, n_rows=S_TOK, gb=GB_XA, t=DEC_SEQ, op_dtype=F32, name="xattn_sample")
        x2 = jnp.concatenate([x2_p, x2_s], axis=0)
        x2b = jnp.concatenate([x2b_p, x2b_s], axis=0)

        x, xb = _moe(x2, x2b, wr, br, w_gate_e[l].astype(BF16), w_up_e[l].astype(BF16),
                     w_down_e[l].astype(BF16), row(ln_g[l, 2]), row(ln_b[l, 2]), TM_MOE)

        outs["ckv_p"].append(ckv[:P_TOK].reshape(BATCH, SEQ, KV_LORA))
        outs["kr_p"].append(kr[:P_TOK].reshape(BATCH, SEQ, QK_ROPE))
        outs["mk_p"].append(mk_p.reshape(BATCH, N_MEM, XA_HEADS, XA_HEAD_DIM))
        outs["mv_p"].append(mv_p.reshape(BATCH, N_MEM, XA_HEADS, XA_HEAD_DIM))
        outs["cv_p"].append(glu[:P_TOK].reshape(BATCH, SEQ, CONV_CH)[:, SEQ - (CONV_K - 1):])
        outs["ckv_s"].append(ckv_s)
        outs["kr_s"].append(kr_s)
        outs["cv_s"].append(jnp.concatenate([state_conv[l][:, DEC_SEQ:], glu_s], axis=1))

    return (x[:P_TOK].reshape(BATCH, SEQ, D_MODEL), x[P_TOK:].reshape(DEC_BATCH, DEC_SEQ, D_MODEL),
            jnp.stack(outs["ckv_p"]), jnp.stack(outs["kr_p"]), jnp.stack(outs["mk_p"]), jnp.stack(outs["mv_p"]),
            jnp.stack(outs["cv_p"]), jnp.stack(outs["ckv_s"]), jnp.stack(outs["kr_s"]), jnp.stack(outs["cv_s"]))
```

```python
import functools

import jax
import jax.numpy as jnp
from jax import lax
from jax.experimental import pallas as pl
from jax.experimental.pallas import tpu as pltpu

D_MODEL = 2048
BATCH = 4
SEQ = 2048
DEPTH = 2
DEC_BATCH = 128
DEC_SEQ = 8
PAGE_SIZE = 128
N_HEADS = 16
QK_NOPE = 128
QK_ROPE = 64
V_HEAD = 128
Q_LORA = 512
KV_LORA = 512
ROPE_THETA = 10000.0
CONV_CH = D_MODEL
CONV_K = 31
N_MEM = 256
XA_HEADS = 4
XA_HEAD_DIM = 128
N_GROUPS = 4
EXP_PER_GROUP = 4
N_EXPERTS = 16
D_EXPERT = 512
LN_EPS = 1e-5
RMS_EPS = 1e-6
DEEPNORM_ALPHA = (2 * DEPTH) ** 0.25
NEG_INF = -1e30

P_TOK = BATCH * SEQ
S_TOK = DEC_BATCH * DEC_SEQ
N_TOK = P_TOK + S_TOK
C1 = Q_LORA + KV_LORA + QK_ROPE
MLA_SCALE = (QK_NOPE + QK_ROPE) ** -0.5
XA_SCALE = XA_HEAD_DIM ** -0.5
HALO = 32
ROUTER_LANES = 128
VMEM_LIMIT = 56 * 1024 * 1024

BF16 = jnp.bfloat16
F32 = jnp.float32


def _cparams(*sem):
    return pltpu.CompilerParams(dimension_semantics=sem, vmem_limit_bytes=VMEM_LIMIT)


def _dot(a, b):
    return jnp.dot(a, b, preferred_element_type=F32)


def _dot_nt(a, b):
    return lax.dot_general(a, b, (((1,), (1,)), ((), ())), preferred_element_type=F32)


def _layer_norm(z, g, b):
    mu = jnp.mean(z, axis=-1, keepdims=True)
    zc = z - mu
    var = jnp.mean(zc * zc, axis=-1, keepdims=True)
    return zc * lax.rsqrt(var + LN_EPS) * g + b


def _sigmoid(x):
    return 1.0 / (1.0 + jnp.exp(-x))


def _mm_body(a_ref, b_ref, o_ref):
    o_ref[...] = _dot(a_ref[...], b_ref[...])


def _mm(a, b, tm, tn):
    m, k = a.shape
    n = b.shape[1]
    return pl.pallas_call(
        _mm_body,
        grid=(m // tm, n // tn),
        in_specs=[pl.BlockSpec((tm, k), lambda i, j: (i, 0)),
                  pl.BlockSpec((k, tn), lambda i, j: (0, j))],
        out_specs=pl.BlockSpec((tm, tn), lambda i, j: (i, j)),
        out_shape=jax.ShapeDtypeStruct((m, n), F32),
        compiler_params=_cparams("parallel", "parallel"),
        name="mm",
    )(a, b)


def _proj1_body(x_ref, w_ref, qg_ref, kg_ref, cos_ref, sin_ref,
                qn_ref, ckv_ref, kr_ref, ckvb_ref, krb_ref):
    acc = _dot(x_ref[...], w_ref[...])
    qc = acc[:, :Q_LORA]
    kvc = acc[:, Q_LORA:Q_LORA + KV_LORA]
    kr = acc[:, Q_LORA + KV_LORA:]
    qn = qc * lax.rsqrt(jnp.mean(qc * qc, axis=-1, keepdims=True) + RMS_EPS) * qg_ref[...]
    qn_ref[...] = qn.astype(BF16)
    ckv = kvc * lax.rsqrt(jnp.mean(kvc * kvc, axis=-1, keepdims=True) + RMS_EPS) * kg_ref[...]
    ckv_ref[...] = ckv
    ckvb_ref[...] = ckv.astype(BF16)
    half = QK_ROPE // 2
    swapped = jnp.concatenate([kr[:, half:], kr[:, :half]], axis=-1)
    rot = kr * cos_ref[...] + swapped * sin_ref[...]
    kr_ref[...] = rot
    krb_ref[...] = rot.astype(BF16)


def _pos_block_map(tm):
    n_p = P_TOK // tm
    per_seq = SEQ // tm
    return lambda i: (jnp.where(i < n_p, i % per_seq, per_seq), 0)


def _proj1(xb, w1, qg, kg, cos64, sin64, tm):
    row = lambda i: (i, 0)
    const = lambda i: (0, 0)
    return pl.pallas_call(
        _proj1_body,
        grid=(N_TOK // tm,),
        in_specs=[pl.BlockSpec((tm, D_MODEL), row),
                  pl.BlockSpec((D_MODEL, C1), const),
                  pl.BlockSpec((1, Q_LORA), const),
                  pl.BlockSpec((1, KV_LORA), const),
                  pl.BlockSpec((tm, QK_ROPE), _pos_block_map(tm)),
                  pl.BlockSpec((tm, QK_ROPE), _pos_block_map(tm))],
        out_specs=[pl.BlockSpec((tm, Q_LORA), row),
                   pl.BlockSpec((tm, KV_LORA), row),
                   pl.BlockSpec((tm, QK_ROPE), row),
                   pl.BlockSpec((tm, KV_LORA), row),
                   pl.BlockSpec((tm, QK_ROPE), row)],
        out_shape=[jax.ShapeDtypeStruct((N_TOK, Q_LORA), BF16),
                   jax.ShapeDtypeStruct((N_TOK, KV_LORA), F32),
                   jax.ShapeDtypeStruct((N_TOK, QK_ROPE), F32),
                   jax.ShapeDtypeStruct((N_TOK, KV_LORA), BF16),
                   jax.ShapeDtypeStruct((N_TOK, QK_ROPE), BF16)],
        compiler_params=_cparams("parallel"),
        name="proj1",
    )(xb, w1, qg, kg, cos64, sin64)


def _glu_body(x_ref, wa_ref, wb_ref, o_ref):
    x = x_ref[...]
    a = _dot(x, wa_ref[...])
    b = _dot(x, wb_ref[...])
    o_ref[...] = a * _sigmoid(b)


def _glu(xb, wa, wb, tm, tn):
    return pl.pallas_call(
        _glu_body,
        grid=(N_TOK // tm, CONV_CH // tn),
        in_specs=[pl.BlockSpec((tm, D_MODEL), lambda i, j: (i, 0)),
                  pl.BlockSpec((D_MODEL, tn), lambda i, j: (0, j)),
                  pl.BlockSpec((D_MODEL, tn), lambda i, j: (0, j))],
        out_specs=pl.BlockSpec((tm, tn), lambda i, j: (i, j)),
        out_shape=jax.ShapeDtypeStruct((N_TOK, CONV_CH), F32),
        compiler_params=_cparams("parallel", "parallel"),
        name="glu",
    )(xb, wa, wb)


CONV_ROWS = 32
CONV_LANES = 512
LN_ROWS = 16


def _conv_taps(win_ref, win_row0, w_ref, b_ref, h_ref, h_row0, rows):
    def chan_chunk(c, carry):
        c0 = pl.multiple_of(c * CONV_LANES, CONV_LANES)
        cs = pl.ds(c0, CONV_LANES)
        acc = jnp.broadcast_to(b_ref[:, cs], (rows, CONV_LANES))
        for k in range(CONV_K):
            acc = acc + w_ref[k:k + 1, cs] * win_ref[win_row0 + k:win_row0 + k + rows, cs]
        h_ref[h_row0:h_row0 + rows, cs] = acc
        return carry
    lax.fori_loop(0, CONV_CH // CONV_LANES, chan_chunk, 0)


def _ln_swish_rows(h_ref, g_ref, be_ref, o_ref, n_rows):
    def row_chunk(r, carry):
        rs = pl.ds(pl.multiple_of(r * LN_ROWS, LN_ROWS), LN_ROWS)
        y = _layer_norm(h_ref[rs, :], g_ref[...], be_ref[...])
        o_ref[rs, :] = (y * _sigmoid(y)).astype(o_ref.dtype)
        return carry
    lax.fori_loop(0, n_rows // LN_ROWS, row_chunk, 0)


def _conv_p_body(main_ref, prev_ref, w_ref, b_ref, g_ref, be_ref, o_ref, win_sc, h_sc, *, tt):
    first = pl.program_id(1) == 0
    win_sc[0:HALO, :] = jnp.where(first, 0.0, prev_ref[...])
    win_sc[HALO:HALO + tt, :] = main_ref[...]
    lead = HALO - (CONV_K - 1)
    for r0 in range(0, tt, CONV_ROWS):
        _conv_taps(win_sc, r0 + lead, w_ref, b_ref, h_sc, r0, CONV_ROWS)
    _ln_swish_rows(h_sc, g_ref, be_ref, o_ref, tt)


def _conv_prompt(glu, w, b, g, be, tt):
    n_t = SEQ // tt
    const = lambda bi, i: (0, 0)
    return pl.pallas_call(
        functools.partial(_conv_p_body, tt=tt),
        grid=(BATCH, n_t),
        in_specs=[pl.BlockSpec((tt, CONV_CH), lambda bi, i: (bi * n_t + i, 0)),
                  pl.BlockSpec((HALO, CONV_CH),
                               lambda bi, i: (jnp.maximum((bi * SEQ + i * tt) // HALO - 1, 0), 0)),
                  pl.BlockSpec((HALO, CONV_CH), const),
                  pl.BlockSpec((1, CONV_CH), const),
                  pl.BlockSpec((1, CONV_CH), const),
                  pl.BlockSpec((1, CONV_CH), const)],
        out_specs=pl.BlockSpec((tt, CONV_CH), lambda bi, i: (bi * n_t + i, 0)),
        out_shape=jax.ShapeDtypeStruct((P_TOK, CONV_CH), BF16),
        scratch_shapes=[pltpu.VMEM((HALO + tt, CONV_CH), F32), pltpu.VMEM((tt, CONV_CH), F32)],
        compiler_params=_cparams("parallel", "parallel"),
        name="conv_prompt",
    )(glu, glu, w, b, g, be)


def _conv_s_body(state_ref, new_ref, w_ref, b_ref, g_ref, be_ref, o_ref, win_sc, h_sc, *, gb):
    hist = CONV_K - 1
    for gi in range(gb):
        win_sc[0:hist, :] = state_ref[gi]
        win_sc[hist:hist + DEC_SEQ, :] = new_ref[gi]
        _conv_taps(win_sc, 0, w_ref, b_ref, h_sc, gi * DEC_SEQ, DEC_SEQ)
    _ln_swish_rows(h_sc, g_ref, be_ref, o_ref, gb * DEC_SEQ)


def _conv_sample(state, glu3, w, b, g, be, gb):
    const = lambda i: (0, 0)
    return pl.pallas_call(
        functools.partial(_conv_s_body, gb=gb),
        grid=(DEC_BATCH // gb,),
        in_specs=[pl.BlockSpec((gb, CONV_K - 1, CONV_CH), lambda i: (i, 0, 0)),
                  pl.BlockSpec((gb, DEC_SEQ, CONV_CH), lambda i: (i, 0, 0)),
                  pl.BlockSpec((HALO, CONV_CH), const),
                  pl.BlockSpec((1, CONV_CH), const),
                  pl.BlockSpec((1, CONV_CH), const),
                  pl.BlockSpec((1, CONV_CH), const)],
        out_specs=pl.BlockSpec((gb * DEC_SEQ, CONV_CH), lambda i: (i, 0)),
        out_shape=jax.ShapeDtypeStruct((S_TOK, CONV_CH), BF16),
        scratch_shapes=[pltpu.VMEM((CONV_K - 1 + DEC_SEQ + 2, CONV_CH), F32),
                        pltpu.VMEM((gb * DEC_SEQ, CONV_CH), F32)],
        compiler_params=_cparams("parallel"),
        name="conv_sample",
    )(state, glu3, w, b, g, be)


def _q_body(qn_ref, wuq_ref, wuk_ref, cos_ref, sin_ref, ql_ref, qr_ref, *, sample):
    tm = qn_ref.shape[0]
    q = _dot(qn_ref[...], wuq_ref[...])
    nope_w = N_HEADS * QK_NOPE
    rope_w = N_HEADS * QK_ROPE
    qr = q[:, nope_w:]
    half = QK_ROPE // 2
    lane = lax.broadcasted_iota(jnp.int32, qr.shape, 1)
    ahead = pltpu.roll(qr, rope_w - half, 1)
    behind = pltpu.roll(qr, half, 1)
    swapped = jnp.where((lane % QK_ROPE) < half, ahead, behind)
    rot = qr * cos_ref[...] + swapped * sin_ref[...]
    for h in range(N_HEADS):
        nope = q[:, h * QK_NOPE:(h + 1) * QK_NOPE].astype(BF16)
        lat = _dot(nope, wuk_ref[h]) * MLA_SCALE
        rot_h = rot[:, h * QK_ROPE:(h + 1) * QK_ROPE] * MLA_SCALE
        if sample:
            ql_ref[:, h, :, :] = lat.reshape(tm // DEC_SEQ, DEC_SEQ, KV_LORA)
            qr_ref[:, h, :, :] = rot_h.reshape(tm // DEC_SEQ, DEC_SEQ, QK_ROPE)
        else:
            ql_ref[0, h] = lat.astype(BF16)
            qr_ref[0, h] = rot_h.astype(BF16)


def _q_prompt(qn, wuq, wuk, cos_q, sin_q, tm):
    n_t = SEQ // tm
    c2 = lambda b, i: (0, 0)
    c3 = lambda b, i: (0, 0, 0)
    return pl.pallas_call(
        functools.partial(_q_body, sample=False),
        grid=(BATCH, n_t),
        in_specs=[pl.BlockSpec((tm, Q_LORA), lambda b, i: (b * n_t + i, 0)),
                  pl.BlockSpec((Q_LORA, N_HEADS * (QK_NOPE + QK_ROPE)), c2),
                  pl.BlockSpec((N_HEADS, QK_NOPE, KV_LORA), c3),
                  pl.BlockSpec((tm, N_HEADS * QK_ROPE), lambda b, i: (i, 0)),
                  pl.BlockSpec((tm, N_HEADS * QK_ROPE), lambda b, i: (i, 0))],
        out_specs=[pl.BlockSpec((1, N_HEADS, tm, KV_LORA), lambda b, i: (b, 0, i, 0)),
                   pl.BlockSpec((1, N_HEADS, tm, QK_ROPE), lambda b, i: (b, 0, i, 0))],
        out_shape=[jax.ShapeDtypeStruct((BATCH, N_HEADS, SEQ, KV_LORA), BF16),
                   jax.ShapeDtypeStruct((BATCH, N_HEADS, SEQ, QK_ROPE), BF16)],
        compiler_params=_cparams("parallel", "parallel"),
        name="q_prompt",
    )(qn, wuq, wuk, cos_q, sin_q)


def _q_sample(qn, wuq, wuk, cos_q, sin_q, tm):
    gb = tm // DEC_SEQ
    pos_blk = SEQ // tm
    c2 = lambda i: (0, 0)
    c3 = lambda i: (0, 0, 0)
    return pl.pallas_call(
        functools.partial(_q_body, sample=True),
        grid=(S_TOK // tm,),
        in_specs=[pl.BlockSpec((tm, Q_LORA), lambda i: (P_TOK // tm + i, 0)),
                  pl.BlockSpec((Q_LORA, N_HEADS * (QK_NOPE + QK_ROPE)), c2),
                  pl.BlockSpec((N_HEADS, QK_NOPE, KV_LORA), c3),
                  pl.BlockSpec((tm, N_HEADS * QK_ROPE), lambda i: (pos_blk, 0)),
                  pl.BlockSpec((tm, N_HEADS * QK_ROPE), lambda i: (pos_blk, 0))],
        out_specs=[pl.BlockSpec((gb, N_HEADS, DEC_SEQ, KV_LORA), lambda i: (i, 0, 0, 0)),
                   pl.BlockSpec((gb, N_HEADS, DEC_SEQ, QK_ROPE), lambda i: (i, 0, 0, 0))],
        out_shape=[jax.ShapeDtypeStruct((DEC_BATCH, N_HEADS, DEC_SEQ, KV_LORA), F32),
                   jax.ShapeDtypeStruct((DEC_BATCH, N_HEADS, DEC_SEQ, QK_ROPE), F32)],
        compiler_params=_cparams("parallel"),
        name="q_sample",
    )(qn, wuq, wuk, cos_q, sin_q)


def _attn_p_body(ql_ref, qr_ref, k_ref, kr_ref, wuv_ref, o_ref, m_sc, l_sc, acc_sc, *, tq, tk):
    qi = pl.program_id(1)
    rows = N_HEADS * tq
    ql = ql_ref[0].reshape(rows, KV_LORA)
    qr = qr_ref[0].reshape(rows, QK_ROPE)
    m_sc[...] = jnp.full(m_sc.shape, NEG_INF, F32)
    l_sc[...] = jnp.zeros(l_sc.shape, F32)
    acc_sc[...] = jnp.zeros(acc_sc.shape, F32)
    n_full = (qi * tq) // tk

    def chunk(j, masked):
        ks = pl.ds(pl.multiple_of(j * tk, tk), tk)
        k = k_ref[0, ks, :]
        kr = kr_ref[0, ks, :]
        s = _dot_nt(ql, k) + _dot_nt(qr, kr)
        if masked:
            qpos = qi * tq + lax.broadcasted_iota(jnp.int32, (tq, tk), 0)
            kpos = j * tk + lax.broadcasted_iota(jnp.int32, (tq, tk), 1)
            s = jnp.where((kpos <= qpos)[None], s.reshape(N_HEADS, tq, tk), NEG_INF).reshape(rows, tk)
        m_prev = m_sc[...]
        m_new = jnp.maximum(m_prev, jnp.max(s, axis=-1, keepdims=True))
        alpha = jnp.exp(m_prev - m_new)
        p = jnp.exp(s - m_new)
        l_sc[...] = alpha * l_sc[...] + jnp.sum(p, axis=-1, keepdims=True)
        acc_sc[...] = alpha * acc_sc[...] + _dot(p.astype(BF16), k)
        m_sc[...] = m_new

    def full_chunk(j, carry):
        chunk(j, False)
        return carry

    lax.fori_loop(0, n_full, full_chunk, 0)
    chunk(n_full, True)
    lat = acc_sc[...] / l_sc[...]
    for h in range(N_HEADS):
        lat_h = lat[h * tq:(h + 1) * tq, :].astype(BF16)
        o_ref[:, h * V_HEAD:(h + 1) * V_HEAD] = _dot(lat_h, wuv_ref[h]).astype(BF16)


def _attn_prompt(ql, qr, ckvb, krb, wuv, tq, tk):
    n_q = SEQ // tq
    rows = N_HEADS * tq
    return pl.pallas_call(
        functools.partial(_attn_p_body, tq=tq, tk=tk),
        grid=(BATCH, n_q),
        in_specs=[pl.BlockSpec((1, N_HEADS, tq, KV_LORA), lambda b, i: (b, 0, i, 0)),
                  pl.BlockSpec((1, N_HEADS, tq, QK_ROPE), lambda b, i: (b, 0, i, 0)),
                  pl.BlockSpec((1, SEQ, KV_LORA), lambda b, i: (b, 0, 0)),
                  pl.BlockSpec((1, SEQ, QK_ROPE), lambda b, i: (b, 0, 0)),
                  pl.BlockSpec((N_HEADS, KV_LORA, V_HEAD), lambda b, i: (0, 0, 0))],
        out_specs=pl.BlockSpec((tq, N_HEADS * V_HEAD), lambda b, i: (b * n_q + i, 0)),
        out_shape=jax.ShapeDtypeStruct((P_TOK, N_HEADS * V_HEAD), BF16),
        scratch_shapes=[pltpu.VMEM((rows, 1), F32), pltpu.VMEM((rows, 1), F32),
                        pltpu.VMEM((rows, KV_LORA), F32)],
        compiler_params=_cparams("parallel", "arbitrary"),
        name="attn_prompt",
    )(ql, qr, ckvb, krb, wuv)


def _attn_s_body(pt_ref, ql_ref, qr_ref, cn_ref, kn_ref, ckv_hbm, krt_hbm, o_ref,
                 m_sc, l_sc, acc_sc, kb_sc, krt_sc, ck_buf, kr_buf, sem, *, pages, nb, n_pages, layer):
    b = pl.program_id(0)
    g = pl.program_id(1)
    n_g = pl.num_programs(1)
    rows = N_HEADS * DEC_SEQ
    n_in = nb * pages
    step = b * n_g + g
    slot = step % 2

    def page_copies(step_b, step_g, buf_slot):
        copies = []
        for a in range(nb):
            for u in range(pages):
                i = a * pages + u
                page = pt_ref[(step_b * nb + a) * n_pages + step_g * pages + u]
                copies.append(pltpu.make_async_copy(ckv_hbm.at[layer, page], ck_buf.at[buf_slot, i],
                                                    sem.at[buf_slot, i]))
                copies.append(pltpu.make_async_copy(krt_hbm.at[layer, page], kr_buf.at[buf_slot, i],
                                                    sem.at[buf_slot, n_in + i]))
        return copies

    @pl.when(step == 0)
    def _():
        for cp in page_copies(b, g, slot):
            cp.start()

    @pl.when(step + 1 < pl.num_programs(0) * n_g)
    def _():
        nxt = step + 1
        for cp in page_copies(nxt // n_g, nxt % n_g, 1 - slot):
            cp.start()

    @pl.when(g == 0)
    def _():
        for a in range(nb):
            ql = ql_ref[a].reshape(rows, KV_LORA)
            qr = qr_ref[a].reshape(rows, QK_ROPE)
            cn = cn_ref[a]
            kn = kn_ref[a]
            s = _dot_nt(ql, cn) + _dot_nt(qr, kn)
            t = lax.broadcasted_iota(jnp.int32, s.shape, 0) % DEC_SEQ
            c = lax.broadcasted_iota(jnp.int32, s.shape, 1)
            s = jnp.where(c <= t, s, NEG_INF)
            m = jnp.max(s, axis=-1, keepdims=True)
            p = jnp.exp(s - m)
            m_sc[a] = m
            l_sc[a] = jnp.sum(p, axis=-1, keepdims=True)
            acc_sc[a] = _dot(p, cn)

    for cp in page_copies(b, g, slot):
        cp.wait()
    for a in range(nb):
        for u in range(pages):
            i = a * pages + u
            kb_sc[a, u * PAGE_SIZE:(u + 1) * PAGE_SIZE, :] = ck_buf[slot, i].astype(BF16)
            krt_sc[a, :, u * PAGE_SIZE:(u + 1) * PAGE_SIZE] = kr_buf[slot, i].astype(BF16)
    for a in range(nb):
        ql = ql_ref[a].reshape(rows, KV_LORA).astype(BF16)
        qr = qr_ref[a].reshape(rows, QK_ROPE).astype(BF16)
        kb = kb_sc[a]
        s = _dot_nt(ql, kb) + _dot(qr, krt_sc[a])
        m_prev = m_sc[a]
        m_new = jnp.maximum(m_prev, jnp.max(s, axis=-1, keepdims=True))
        alpha = jnp.exp(m_prev - m_new)
        p = jnp.exp(s - m_new)
        l_sc[a] = alpha * l_sc[a] + jnp.sum(p, axis=-1, keepdims=True)
        acc_sc[a] = alpha * acc_sc[a] + _dot(p.astype(BF16), kb)
        m_sc[a] = m_new

    @pl.when(g == pl.num_programs(1) - 1)
    def _():
        for a in range(nb):
            o_ref[a] = (acc_sc[a] / l_sc[a]).reshape(N_HEADS, DEC_SEQ, KV_LORA)


def _attn_sample(page_table, ql, qr, ckv_new, kr_new, cache_ckv, cache_krope_t, layer, pages, nb):
    n_pages = page_table.shape[1]
    n_g = n_pages // pages
    rows = N_HEADS * DEC_SEQ
    n_in = nb * pages
    per_b4 = lambda b, g, pt: (b, 0, 0, 0)
    per_b3 = lambda b, g, pt: (b, 0, 0)
    grid_spec = pltpu.PrefetchScalarGridSpec(
        num_scalar_prefetch=1,
        grid=(DEC_BATCH // nb, n_g),
        in_specs=[pl.BlockSpec((nb, N_HEADS, DEC_SEQ, KV_LORA), per_b4),
                  pl.BlockSpec((nb, N_HEADS, DEC_SEQ, QK_ROPE), per_b4),
                  pl.BlockSpec((nb, DEC_SEQ, KV_LORA), per_b3),
                  pl.BlockSpec((nb, DEC_SEQ, QK_ROPE), per_b3),
                  pl.BlockSpec(memory_space=pl.ANY),
                  pl.BlockSpec(memory_space=pl.ANY)],
        out_specs=pl.BlockSpec((nb, N_HEADS, DEC_SEQ, KV_LORA), per_b4),
        scratch_shapes=[pltpu.VMEM((nb, rows, 1), F32), pltpu.VMEM((nb, rows, 1), F32),
                        pltpu.VMEM((nb, rows, KV_LORA), F32),
                        pltpu.VMEM((nb, pages * PAGE_SIZE, KV_LORA), BF16),
                        pltpu.VMEM((nb, QK_ROPE, pages * PAGE_SIZE), BF16),
                        pltpu.VMEM((2, n_in, PAGE_SIZE, KV_LORA), F32),
                        pltpu.VMEM((2, n_in, QK_ROPE, PAGE_SIZE), F32),
                        pltpu.SemaphoreType.DMA((2, 2 * n_in))])
    return pl.pallas_call(
        functools.partial(_attn_s_body, pages=pages, nb=nb, n_pages=n_pages, layer=layer),
        grid_spec=grid_spec,
        out_shape=jax.ShapeDtypeStruct((DEC_BATCH, N_HEADS, DEC_SEQ, KV_LORA), F32),
        compiler_params=_cparams("arbitrary", "arbitrary"),
        name="attn_sample",
    )(page_table.reshape(-1), ql, qr, ckv_new, kr_new, cache_ckv, cache_krope_t)


def _vup_s_body(lat_ref, wuv_ref, o_ref):
    lat = lat_ref[...].reshape(S_TOK, KV_LORA).astype(BF16)
    o_ref[...] = _dot(lat, wuv_ref[0]).astype(BF16)


def _vup_sample(lat, wuv):
    return pl.pallas_call(
        _vup_s_body,
        grid=(N_HEADS,),
        in_specs=[pl.BlockSpec((DEC_BATCH, None, DEC_SEQ, KV_LORA), lambda h: (0, h, 0, 0)),
                  pl.BlockSpec((1, KV_LORA, V_HEAD), lambda h: (h, 0, 0))],
        out_specs=pl.BlockSpec((S_TOK, V_HEAD), lambda h: (0, h)),
        out_shape=jax.ShapeDtypeStruct((S_TOK, N_HEADS * V_HEAD), BF16),
        compiler_params=_cparams("parallel"),
        name="vup_sample",
    )(lat, wuv)


def _merge_body(vp_ref, vs_ref, hp_ref, hs_ref, x_ref, wo_ref, wp_ref, wga_ref, wgb_ref, ba_ref, bb_ref,
                o_ref, v_sc, h_sc, *, n_p):
    i = pl.program_id(0)
    j = pl.program_id(1)

    @pl.when((j == 0) & (i < n_p))
    def _():
        v_sc[...] = vp_ref[...]
        h_sc[...] = hp_ref[...]

    @pl.when((j == 0) & (i >= n_p))
    def _():
        v_sc[...] = vs_ref[...]
        h_sc[...] = hs_ref[...]

    x = x_ref[...]
    branch_a = _dot(v_sc[...], wo_ref[...])
    branch_b = _dot(h_sc[...], wp_ref[...])
    gate_a = _sigmoid(_dot(x, wga_ref[...]) + ba_ref[...])
    gate_b = _sigmoid(_dot(x, wgb_ref[...]) + bb_ref[...])
    o_ref[...] = (gate_a * branch_a + gate_b * branch_b).astype(BF16)


def _merge(v_p, v_s, hb_p, hb_s, xb, wo, wp, wg, bg, tm, tn):
    n_j = D_MODEL // tn
    n_p = P_TOK // tm
    row = lambda i, j: (i, 0)
    row_p = lambda i, j: (jnp.minimum(i, n_p - 1), 0)
    row_s = lambda i, j: (jnp.maximum(i - n_p, 0), 0)
    col = lambda i, j: (0, j)
    col_b = lambda i, j: (0, j + n_j)
    return pl.pallas_call(
        functools.partial(_merge_body, n_p=n_p),
        grid=(N_TOK // tm, n_j),
        in_specs=[pl.BlockSpec((tm, D_MODEL), row_p),
                  pl.BlockSpec((tm, D_MODEL), row_s),
                  pl.BlockSpec((tm, D_MODEL), row_p),
                  pl.BlockSpec((tm, D_MODEL), row_s),
                  pl.BlockSpec((tm, D_MODEL), row),
                  pl.BlockSpec((D_MODEL, tn), col),
                  pl.BlockSpec((D_MODEL, tn), col),
                  pl.BlockSpec((D_MODEL, tn), col),
                  pl.BlockSpec((D_MODEL, tn), col_b),
                  pl.BlockSpec((1, tn), col),
                  pl.BlockSpec((1, tn), col_b)],
        out_specs=pl.BlockSpec((tm, tn), lambda i, j: (i, j)),
        out_shape=jax.ShapeDtypeStruct((N_TOK, D_MODEL), BF16),
        scratch_shapes=[pltpu.VMEM((tm, D_MODEL), BF16), pltpu.VMEM((tm, D_MODEL), BF16)],
        compiler_params=_cparams("parallel", "arbitrary"),
        name="merge",
    )(v_p, v_s, hb_p, hb_s, xb, wo, wp, wg, wg, bg, bg)


def _mm_res_ln_body(a_ref, w_ref, res_ref, g_ref, b_ref, o_ref, ob_ref):
    z = DEEPNORM_ALPHA * res_ref[...] + _dot(a_ref[...], w_ref[...])
    y = _layer_norm(z, g_ref[...], b_ref[...])
    o_ref[...] = y
    ob_ref[...] = y.astype(BF16)


def _mm_res_ln(a, w, res, g, b, tm):
    k = a.shape[1]
    row = lambda i: (i, 0)
    const = lambda i: (0, 0)
    return pl.pallas_call(
        _mm_res_ln_body,
        grid=(N_TOK // tm,),
        in_specs=[pl.BlockSpec((tm, k), row),
                  pl.BlockSpec((k, D_MODEL), const),
                  pl.BlockSpec((tm, D_MODEL), row),
                  pl.BlockSpec((1, D_MODEL), const),
                  pl.BlockSpec((1, D_MODEL), const)],
        out_specs=[pl.BlockSpec((tm, D_MODEL), row), pl.BlockSpec((tm, D_MODEL), row)],
        out_shape=[jax.ShapeDtypeStruct((N_TOK, D_MODEL), F32),
                   jax.ShapeDtypeStruct((N_TOK, D_MODEL), BF16)],
        compiler_params=_cparams("parallel"),
        name="mm_res_ln",
    )(a, w, res, g, b)


def _xattn_tail(o, x_ref, wo_ref, g_ref, b_ref, wr_ref, br_ref, o_ref, ob_ref, lg_ref):
    z = DEEPNORM_ALPHA * x_ref[...] + _dot(o, wo_ref[...])
    y = _layer_norm(z, g_ref[...], b_ref[...])
    o_ref[...] = y
    ob_ref[...] = y.astype(BF16)
    lg_ref[...] = jnp.dot(y, wr_ref[...], preferred_element_type=F32,
                          precision=lax.Precision.HIGHEST) + br_ref[...]


def _xattn_p_body(x_ref, xb_ref, wq_ref, mk_ref, mv_ref, wo_ref, g_ref, b_ref, wr_ref, br_ref,
                  o_ref, ob_ref, lg_ref):
    q = _dot(xb_ref[...], wq_ref[...])
    heads = []
    for h in range(XA_HEADS):
        hs = slice(h * XA_HEAD_DIM, (h + 1) * XA_HEAD_DIM)
        kh = mk_ref[0, :, hs].astype(BF16)
        vh = mv_ref[0, :, hs].astype(BF16)
        s = _dot_nt(q[:, hs].astype(BF16), kh) * XA_SCALE
        e = jnp.exp(s - jnp.max(s, axis=-1, keepdims=True))
        p = e / jnp.sum(e, axis=-1, keepdims=True)
        heads.append(_dot(p.astype(BF16), vh))
    o = jnp.concatenate(heads, axis=-1).astype(BF16)
    _xattn_tail(o, x_ref, wo_ref, g_ref, b_ref, wr_ref, br_ref, o_ref, ob_ref, lg_ref)


def _xattn_s_body(x_ref, xb_ref, wq_ref, mk_ref, mv_ref, wo_ref, g_ref, b_ref, wr_ref, br_ref,
                  o_ref, ob_ref, lg_ref, *, gb):
    q = _dot(xb_ref[...], wq_ref[...])
    rows = XA_HEADS * DEC_SEQ
    keys = N_MEM * XA_HEADS
    row_head = lax.broadcasted_iota(jnp.int32, (rows, keys), 0) // DEC_SEQ
    key_head = lax.broadcasted_iota(jnp.int32, (rows, keys), 1) % XA_HEADS
    own = row_head == key_head
    outs = []
    for gi in range(gb):
        qg = q[gi * DEC_SEQ:(gi + 1) * DEC_SEQ, :]
        q4 = jnp.concatenate([qg[:, h * XA_HEAD_DIM:(h + 1) * XA_HEAD_DIM] for h in range(XA_HEADS)],
                             axis=0).astype(BF16)
        k = mk_ref[gi].astype(BF16)
        v = mv_ref[gi].astype(BF16)
        s = jnp.where(own, _dot_nt(q4, k) * XA_SCALE, NEG_INF)
        e = jnp.exp(s - jnp.max(s, axis=-1, keepdims=True))
        p = e / jnp.sum(e, axis=-1, keepdims=True)
        o4 = _dot(p.astype(BF16), v)
        outs.append(jnp.concatenate([o4[h * DEC_SEQ:(h + 1) * DEC_SEQ, :] for h in range(XA_HEADS)], axis=1))
    o = jnp.concatenate(outs, axis=0).astype(BF16)
    _xattn_tail(o, x_ref, wo_ref, g_ref, b_ref, wr_ref, br_ref, o_ref, ob_ref, lg_ref)


def _xattn(body, x, xb, wq, mk, mv, wo, g, b, wr, br, *, row0, n_rows, tm, mem_spec, name):
    blk0 = row0 // tm
    hd = XA_HEADS * XA_HEAD_DIM
    row = lambda i: (blk0 + i, 0)
    out_row = lambda i: (i, 0)
    const = lambda i: (0, 0)
    return pl.pallas_call(
        body,
        grid=(n_rows // tm,),
        in_specs=[pl.BlockSpec((tm, D_MODEL), row),
                  pl.BlockSpec((tm, D_MODEL), row),
                  pl.BlockSpec((D_MODEL, hd), const),
                  mem_spec,
                  mem_spec,
                  pl.BlockSpec((hd, D_MODEL), const),
                  pl.BlockSpec((1, D_MODEL), const),
                  pl.BlockSpec((1, D_MODEL), const),
                  pl.BlockSpec((D_MODEL, ROUTER_LANES), const),
                  pl.BlockSpec((1, ROUTER_LANES), const)],
        out_specs=[pl.BlockSpec((tm, D_MODEL), out_row), pl.BlockSpec((tm, D_MODEL), out_row),
                   pl.BlockSpec((tm, ROUTER_LANES), out_row)],
        out_shape=[jax.ShapeDtypeStruct((n_rows, D_MODEL), F32),
                   jax.ShapeDtypeStruct((n_rows, D_MODEL), BF16),
                   jax.ShapeDtypeStruct((n_rows, ROUTER_LANES), F32)],
        compiler_params=_cparams("parallel"),
        name=name,
    )(x, xb, wq, mk, mv, wo, g, b, wr, br)


def _route(logits):
    lane = lax.broadcasted_iota(jnp.int32, logits.shape, 1)
    big = jnp.int32(ROUTER_LANES)
    gl = jnp.where(lane < N_GROUPS, logits, NEG_INF)
    gmax = jnp.max(gl, axis=-1, keepdims=True)
    g_sel = jnp.min(jnp.where(gl == gmax, lane, big), axis=-1, keepdims=True)
    g_w = 1.0 / jnp.sum(jnp.exp(gl - gmax), axis=-1, keepdims=True)
    lo = N_GROUPS + EXP_PER_GROUP * g_sel
    el = jnp.where((lane >= lo) & (lane < lo + EXP_PER_GROUP), logits, NEG_INF)
    v1 = jnp.max(el, axis=-1, keepdims=True)
    i1 = jnp.min(jnp.where(el == v1, lane, big), axis=-1, keepdims=True)
    el2 = jnp.where(lane == i1, NEG_INF, el)
    v2 = jnp.max(el2, axis=-1, keepdims=True)
    i2 = jnp.min(jnp.where(el2 == v2, lane, big), axis=-1, keepdims=True)
    e2 = jnp.exp(v2 - v1)
    w1 = g_w / (1.0 + e2)
    w2 = g_w * e2 / (1.0 + e2)
    return jnp.where(lane == i1, w1, 0.0) + jnp.where(lane == i2, w2, 0.0)


MOE_TILE = 768
MOE_CHUNK = 128


def _moe_body(lg_ref, xb_ref, tri_ref, wg_ref, wu_ref, wd_ref, o_ref, comb_sc, combt_sc, rank_sc, rankt_sc):
    e = pl.program_id(1)
    tm = xb_ref.shape[0]

    @pl.when(e == 0)
    def _():
        comb = _route(lg_ref[...])
        combt = comb.T
        comb_sc[...] = comb
        combt_sc[...] = combt
        chosen = jnp.where(combt > 0.0, 1.0, 0.0).astype(BF16)
        rank = _dot(chosen, tri_ref[...])
        rank_sc[...] = rank
        rankt_sc[...] = rank.T
        o_ref[...] = jnp.zeros(o_ref.shape, F32)

    le = N_GROUPS + e
    w_row = combt_sc[pl.ds(le, 1), :]
    r_row = rank_sc[pl.ds(le, 1), :]
    lane = lax.broadcasted_iota(jnp.int32, (tm, ROUTER_LANES), 1)
    w_col = jnp.sum(jnp.where(lane == le, comb_sc[...], 0.0), axis=-1, keepdims=True)
    r_col = jnp.sum(jnp.where(lane == le, rankt_sc[...], 0.0), axis=-1, keepdims=True)
    count = jnp.sum(jnp.where(w_row > 0.0, 1.0, 0.0)).astype(jnp.int32)
    slot_r = lax.broadcasted_iota(jnp.int32, (MOE_CHUNK, tm), 0).astype(F32)
    slot_c = lax.broadcasted_iota(jnp.int32, (tm, MOE_CHUNK), 1).astype(F32)

    def chunk(ci, carry):
        off = (ci * MOE_CHUNK).astype(F32)
        hit = (r_row == slot_r + off) & (w_row > 0.0)
        xg = _dot(jnp.where(hit, 1.0, 0.0).astype(BF16), xb_ref[...]).astype(BF16)
        cg = jnp.sum(jnp.where(hit, w_row, 0.0), axis=-1, keepdims=True)
        gate = _dot(xg, wg_ref[...])
        up = _dot(xg, wu_ref[...])
        hid = (gate * _sigmoid(gate)) * up * cg
        y = _dot(hid.astype(BF16), wd_ref[...]).astype(BF16)
        hit_t = (r_col == slot_c + off) & (w_col > 0.0)
        o_ref[...] += _dot(jnp.where(hit_t, 1.0, 0.0).astype(BF16), y)
        return carry

    lax.fori_loop(0, (count + MOE_CHUNK - 1) // MOE_CHUNK, chunk, 0)


def _moe(logits, xb, tri, wg, wu, wd):
    tm = MOE_TILE
    row = lambda i, e: (i, 0)
    const = lambda i, e: (0, 0)
    return pl.pallas_call(
        _moe_body,
        grid=(N_TOK // tm, N_EXPERTS),
        in_specs=[pl.BlockSpec((tm, ROUTER_LANES), row),
                  pl.BlockSpec((tm, D_MODEL), row),
                  pl.BlockSpec((tm, tm), const),
                  pl.BlockSpec((None, D_MODEL, D_EXPERT), lambda i, e: (e, 0, 0)),
                  pl.BlockSpec((None, D_MODEL, D_EXPERT), lambda i, e: (e, 0, 0)),
                  pl.BlockSpec((None, D_EXPERT, D_MODEL), lambda i, e: (e, 0, 0))],
        out_specs=pl.BlockSpec((tm, D_MODEL), row),
        out_shape=jax.ShapeDtypeStruct((N_TOK, D_MODEL), F32),
        scratch_shapes=[pltpu.VMEM((tm, ROUTER_LANES), F32), pltpu.VMEM((ROUTER_LANES, tm), F32),
                        pltpu.VMEM((ROUTER_LANES, tm), F32), pltpu.VMEM((tm, ROUTER_LANES), F32)],
        compiler_params=_cparams("parallel", "arbitrary"),
        name="moe",
    )(logits, xb, tri, wg, wu, wd)


def _add_ln_body(xp_ref, xs_ref, acc_ref, g_ref, b_ref, o_ref, ob_ref, *, n_p):
    x = jnp.where(pl.program_id(0) < n_p, xp_ref[...], xs_ref[...])
    y = _layer_norm(DEEPNORM_ALPHA * x + acc_ref[...], g_ref[...], b_ref[...])
    o_ref[...] = y
    ob_ref[...] = y.astype(BF16)


def _add_ln(x_p, x_s, acc, g, b, tm):
    n_p = P_TOK // tm
    row = lambda i: (i, 0)
    const = lambda i: (0, 0)
    return pl.pallas_call(
        functools.partial(_add_ln_body, n_p=n_p),
        grid=(N_TOK // tm,),
        in_specs=[pl.BlockSpec((tm, D_MODEL), lambda i: (jnp.minimum(i, n_p - 1), 0)),
                  pl.BlockSpec((tm, D_MODEL), lambda i: (jnp.maximum(i - n_p, 0), 0)),
                  pl.BlockSpec((tm, D_MODEL), row),
                  pl.BlockSpec((1, D_MODEL), const),
                  pl.BlockSpec((1, D_MODEL), const)],
        out_specs=[pl.BlockSpec((tm, D_MODEL), row), pl.BlockSpec((tm, D_MODEL), row)],
        out_shape=[jax.ShapeDtypeStruct((N_TOK, D_MODEL), F32),
                   jax.ShapeDtypeStruct((N_TOK, D_MODEL), BF16)],
        compiler_params=_cparams("parallel"),
        name="add_ln",
    )(x_p, x_s, acc, g, b)


def _rotary(pos):
    half = QK_ROPE // 2
    inv = ROPE_THETA ** (-jnp.arange(half, dtype=F32) / half)
    ang = pos.astype(F32)[:, None] * inv[None, :]
    cos, sin = jnp.cos(ang), jnp.sin(ang)
    return jnp.concatenate([cos, cos], axis=-1), jnp.concatenate([-sin, sin], axis=-1)


TM_PROJ = 512
TM_Q = 256
TQ_ATTN = 128
TK_ATTN = 512
TT_CONV = 128
GB_CONV = 8
GB_XA = 8
PAGES_PER_STEP = 16
ELEMS_PER_STEP = 2
TM_LN = 256
TM_ADD_LN = 512


def kernel(x_prompt, x_sample, mem_prompt, cache_ckv, cache_krope, cache_mem_k, cache_mem_v, state_conv, page_table, w_in, b_gate, q_norm_g, w_uq, kv_norm_g, w_uk, w_uv, w_o_mla, conv_w, conv_b, conv_ln_g, conv_ln_b, w_pw2, w_out, xa_wq, xa_wk, xa_wv, xa_wo, router_group_w, router_group_b, router_expert_w, router_expert_b, w_gate_e, w_up_e, w_down_e, ln_g, ln_b):
    n_pages = page_table.shape[1]
    past_len = n_pages * PAGE_SIZE
    hd = XA_HEADS * XA_HEAD_DIM

    pos_rows = max(TM_PROJ, TM_Q)
    pos = jnp.concatenate([jnp.arange(SEQ, dtype=jnp.int32),
                           past_len + (jnp.arange(pos_rows, dtype=jnp.int32) % DEC_SEQ)])
    cos64, sin64 = _rotary(pos)
    cos_q = jnp.tile(cos64, (1, N_HEADS))
    sin_q = jnp.tile(sin64, (1, N_HEADS))
    n_pos_k = SEQ + TM_PROJ
    n_pos_q = SEQ + TM_Q

    x = jnp.concatenate([x_prompt.reshape(P_TOK, D_MODEL), x_sample.reshape(S_TOK, D_MODEL)], axis=0)
    xb = x.astype(BF16)

    w_mem = jnp.concatenate([xa_wk.reshape(DEPTH, D_MODEL, hd), xa_wv.reshape(DEPTH, D_MODEL, hd)], axis=-1)
    w_mem = jnp.moveaxis(w_mem, 0, 1).reshape(D_MODEL, DEPTH * 2 * hd).astype(BF16)
    mem_kv = _mm(mem_prompt.reshape(BATCH * N_MEM, D_MODEL).astype(BF16), w_mem, 512, 512)
    mem_kv = mem_kv.reshape(BATCH, N_MEM, DEPTH, 2, hd)

    cache_krope_t = jnp.swapaxes(cache_krope, 2, 3)
    mem_k_rows = cache_mem_k.reshape(DEPTH, DEC_BATCH, N_MEM * XA_HEADS, XA_HEAD_DIM)
    mem_v_rows = cache_mem_v.reshape(DEPTH, DEC_BATCH, N_MEM * XA_HEADS, XA_HEAD_DIM)
    tri = jnp.triu(jnp.ones((MOE_TILE, MOE_TILE), F32), k=1).astype(BF16)

    outs ={k: [] for k in ("ckv_p", "kr_p", "mk_p", "mv_p", "cv_p", "ckv_s", "kr_s", "cv_s")}
    for l in range(DEPTH):
        wl = w_in[l]
        w1 = wl[:, :C1].astype(BF16)
        wa = wl[:, C1:C1 + CONV_CH].astype(BF16)
        wb = wl[:, C1 + CONV_CH:C1 + 2 * CONV_CH].astype(BF16)
        wg = wl[:, C1 + 2 * CONV_CH:].astype(BF16)
        wuq = jnp.concatenate([w_uq[l][:, :, :QK_NOPE].reshape(Q_LORA, N_HEADS * QK_NOPE),
                               w_uq[l][:, :, QK_NOPE:].reshape(Q_LORA, N_HEADS * QK_ROPE)], axis=-1).astype(BF16)
        wuk = jnp.transpose(w_uk[l], (1, 2, 0)).astype(BF16)
        wuv = jnp.transpose(w_uv[l], (1, 0, 2)).astype(BF16)
        wo = w_o_mla[l].reshape(N_HEADS * V_HEAD, D_MODEL).astype(BF16)
        wp = w_pw2[l].astype(BF16)
        wout = w_out[l].astype(BF16)
        wxq = xa_wq[l].reshape(D_MODEL, hd).astype(BF16)
        wxo = xa_wo[l].reshape(hd, D_MODEL).astype(BF16)
        wr = jnp.concatenate([router_group_w[l], router_expert_w[l]], axis=-1)
        wr = jnp.pad(wr, ((0, 0), (0, ROUTER_LANES - wr.shape[1])))
        br = jnp.concatenate([router_group_b[l], router_expert_b[l]])
        br = jnp.pad(br, (0, ROUTER_LANES - br.shape[0]))[None, :]
        conv_w_l = jnp.pad(conv_w[l], ((0, HALO - CONV_K), (0, 0)))
        row = lambda v: v[None, :]

        qn, ckv, kr, ckvb, krb = _proj1(xb, w1, row(q_norm_g[l]), row(kv_norm_g[l]),
                                        cos64[:n_pos_k], sin64[:n_pos_k], TM_PROJ)
        glu = _glu(xb, wa, wb, 512, 512)

        hb_p = _conv_prompt(glu, conv_w_l, row(conv_b[l]), row(conv_ln_g[l]), row(conv_ln_b[l]), TT_CONV)
        glu_s = glu[P_TOK:].reshape(DEC_BATCH, DEC_SEQ, CONV_CH)
        hb_s = _conv_sample(state_conv[l], glu_s, conv_w_l, row(conv_b[l]), row(conv_ln_g[l]),
                            row(conv_ln_b[l]), GB_CONV)

        ql_p, qr_p = _q_prompt(qn, wuq, wuk, cos_q[:n_pos_q], sin_q[:n_pos_q], TM_Q)
        ql_s, qr_s = _q_sample(qn, wuq, wuk, cos_q[:n_pos_q], sin_q[:n_pos_q], TM_Q)
        v_p = _attn_prompt(ql_p, qr_p, ckvb[:P_TOK].reshape(BATCH, SEQ, KV_LORA),
                           krb[:P_TOK].reshape(BATCH, SEQ, QK_ROPE), wuv, TQ_ATTN, TK_ATTN)
        ckv_s = ckv[P_TOK:].reshape(DEC_BATCH, DEC_SEQ, KV_LORA)
        kr_s = kr[P_TOK:].reshape(DEC_BATCH, DEC_SEQ, QK_ROPE)
        lat_s = _attn_sample(page_table, ql_s, qr_s, ckv_s, kr_s, cache_ckv, cache_krope_t, l,
                             PAGES_PER_STEP, ELEMS_PER_STEP)
        v_s = _vup_sample(lat_s, wuv)

        merged = _merge(v_p, v_s, hb_p, hb_s, xb, wo, wp, wg, row(b_gate[l]), 512, 512)
        x1, x1b = _mm_res_ln(merged, wout, x, row(ln_g[l, 0]), row(ln_b[l, 0]), TM_LN)

        mk_p = mem_kv[:, :, l, 0]
        mv_p = mem_kv[:, :, l, 1]
        xa_args = (wxo, row(ln_g[l, 1]), row(ln_b[l, 1]), wr, br)
        x2_p, x2b_p, lg_p = _xattn(
            _xattn_p_body, x1, x1b, wxq, mk_p, mv_p, *xa_args, row0=0, n_rows=P_TOK, tm=512,
            mem_spec=pl.BlockSpec((1, N_MEM, hd), lambda i: (i // (SEQ // 512), 0, 0)), name="xattn_prompt")
        x2_s, x2b_s, lg_s = _xattn(
            functools.partial(_xattn_s_body, gb=GB_XA), x1, x1b, wxq, mem_k_rows, mem_v_rows, *xa_args,
            row0=P_TOK, n_rows=S_TOK, tm=GB_XA * DEC_SEQ,
            mem_spec=pl.BlockSpec((None, GB_XA, N_MEM * XA_HEADS, XA_HEAD_DIM), lambda i, l=l: (l, i, 0, 0)),
            name="xattn_sample")

        moe_out = _moe(jnp.concatenate([lg_p, lg_s], axis=0), jnp.concatenate([x2b_p, x2b_s], axis=0), tri,
                       w_gate_e[l].astype(BF16), w_up_e[l].astype(BF16), w_down_e[l].astype(BF16))
        x, xb = _add_ln(x2_p, x2_s, moe_out, row(ln_g[l, 2]), row(ln_b[l, 2]), TM_ADD_LN)

        outs["ckv_p"].append(ckv[:P_TOK].reshape(BATCH, SEQ, KV_LORA))
        outs["kr_p"].append(kr[:P_TOK].reshape(BATCH, SEQ, QK_ROPE))
        outs["mk_p"].append(mk_p.reshape(BATCH, N_MEM, XA_HEADS, XA_HEAD_DIM))
        outs["mv_p"].append(mv_p.reshape(BATCH, N_MEM, XA_HEADS, XA_HEAD_DIM))
        outs["cv_p"].append(glu[:P_TOK].reshape(BATCH, SEQ, CONV_CH)[:, SEQ - (CONV_K - 1):])
        outs["ckv_s"].append(ckv_s)
        outs["kr_s"].append(kr_s)
        outs["cv_s"].append(jnp.concatenate([state_conv[l][:, DEC_SEQ:], glu_s], axis=1))

    return (x[:P_TOK].reshape(BATCH, SEQ, D_MODEL), x[P_TOK:].reshape(DEC_BATCH, DEC_SEQ, D_MODEL),
            jnp.stack(outs["ckv_p"]), jnp.stack(outs["kr_p"]), jnp.stack(outs["mk_p"]), jnp.stack(outs["mv_p"]),
            jnp.stack(outs["cv_p"]), jnp.stack(outs["ckv_s"]), jnp.stack(outs["kr_s"]), jnp.stack(outs["cv_s"]))
```

```python
import functools

import jax
import jax.numpy as jnp
from jax import lax
from jax.experimental import pallas as pl
from jax.experimental.pallas import tpu as pltpu

D_MODEL = 2048
BATCH = 4
SEQ = 2048
DEPTH = 2
DEC_BATCH = 128
DEC_SEQ = 8
PAGE_SIZE = 128
N_HEADS = 16
QK_NOPE = 128
QK_ROPE = 64
V_HEAD = 128
Q_LORA = 512
KV_LORA = 512
ROPE_THETA = 10000.0
CONV_CH = D_MODEL
CONV_K = 31
N_MEM = 256
XA_HEADS = 4
XA_HEAD_DIM = 128
N_GROUPS = 4
EXP_PER_GROUP = 4
N_EXPERTS = 16
D_EXPERT = 512
LN_EPS = 1e-5
RMS_EPS = 1e-6
DEEPNORM_ALPHA = (2 * DEPTH) ** 0.25
NEG_INF = -1e30

P_TOK = BATCH * SEQ
S_TOK = DEC_BATCH * DEC_SEQ
N_TOK = P_TOK + S_TOK
C1 = Q_LORA + KV_LORA + QK_ROPE
MLA_SCALE = (QK_NOPE + QK_ROPE) ** -0.5
XA_SCALE = XA_HEAD_DIM ** -0.5
HALO = 32
ROUTER_LANES = 128
VMEM_LIMIT = 56 * 1024 * 1024

BF16 = jnp.bfloat16
F32 = jnp.float32


def _cparams(*sem):
    return pltpu.CompilerParams(dimension_semantics=sem, vmem_limit_bytes=VMEM_LIMIT)


def _dot(a, b):
    return jnp.dot(a, b, preferred_element_type=F32)


def _dot_nt(a, b):
    return lax.dot_general(a, b, (((1,), (1,)), ((), ())), preferred_element_type=F32)


def _layer_norm(z, g, b):
    mu = jnp.mean(z, axis=-1, keepdims=True)
    zc = z - mu
    var = jnp.mean(zc * zc, axis=-1, keepdims=True)
    return zc * lax.rsqrt(var + LN_EPS) * g + b


def _sigmoid(x):
    return 1.0 / (1.0 + jnp.exp(-x))


def _mm_body(a_ref, b_ref, o_ref):
    o_ref[...] = _dot(a_ref[...], b_ref[...])


def _mm(a, b, tm, tn):
    m, k = a.shape
    n = b.shape[1]
    return pl.pallas_call(
        _mm_body,
        grid=(m // tm, n // tn),
        in_specs=[pl.BlockSpec((tm, k), lambda i, j: (i, 0)),
                  pl.BlockSpec((k, tn), lambda i, j: (0, j))],
        out_specs=pl.BlockSpec((tm, tn), lambda i, j: (i, j)),
        out_shape=jax.ShapeDtypeStruct((m, n), F32),
        compiler_params=_cparams("parallel", "parallel"),
        name="mm",
    )(a, b)


QK_CAT = KV_LORA + 2 * QK_ROPE


def _proj1_body(x_ref, w_ref, qg_ref, kg_ref, cos_ref, sin_ref,
                qn_ref, ckv_ref, kr_ref, kcat_ref):
    acc = _dot(x_ref[...], w_ref[...])
    qc = acc[:, :Q_LORA]
    kvc = acc[:, Q_LORA:Q_LORA + KV_LORA]
    kr = acc[:, Q_LORA + KV_LORA:]
    qn = qc * lax.rsqrt(jnp.mean(qc * qc, axis=-1, keepdims=True) + RMS_EPS) * qg_ref[...]
    qn_ref[...] = qn.astype(BF16)
    ckv = kvc * lax.rsqrt(jnp.mean(kvc * kvc, axis=-1, keepdims=True) + RMS_EPS) * kg_ref[...]
    ckv_ref[...] = ckv
    half = QK_ROPE // 2
    swapped = jnp.concatenate([kr[:, half:], kr[:, :half]], axis=-1)
    rot = kr * cos_ref[...] + swapped * sin_ref[...]
    kr_ref[...] = rot
    kcat_ref[:, :KV_LORA] = ckv.astype(BF16)
    kcat_ref[:, KV_LORA:KV_LORA + QK_ROPE] = rot.astype(BF16)
    kcat_ref[:, KV_LORA + QK_ROPE:] = jnp.zeros((rot.shape[0], QK_CAT - KV_LORA - QK_ROPE), BF16)


def _pos_block_map(tm):
    n_p = P_TOK // tm
    per_seq = SEQ // tm
    return lambda i: (jnp.where(i < n_p, i % per_seq, per_seq), 0)


def _proj1(xb, w1, qg, kg, cos64, sin64, tm):
    row = lambda i: (i, 0)
    const = lambda i: (0, 0)
    return pl.pallas_call(
        _proj1_body,
        grid=(N_TOK // tm,),
        in_specs=[pl.BlockSpec((tm, D_MODEL), row),
                  pl.BlockSpec((D_MODEL, C1), const),
                  pl.BlockSpec((1, Q_LORA), const),
                  pl.BlockSpec((1, KV_LORA), const),
                  pl.BlockSpec((tm, QK_ROPE), _pos_block_map(tm)),
                  pl.BlockSpec((tm, QK_ROPE), _pos_block_map(tm))],
        out_specs=[pl.BlockSpec((tm, Q_LORA), row),
                   pl.BlockSpec((tm, KV_LORA), row),
                   pl.BlockSpec((tm, QK_ROPE), row),
                   pl.BlockSpec((tm, QK_CAT), row)],
        out_shape=[jax.ShapeDtypeStruct((N_TOK, Q_LORA), BF16),
                   jax.ShapeDtypeStruct((N_TOK, KV_LORA), F32),
                   jax.ShapeDtypeStruct((N_TOK, QK_ROPE), F32),
                   jax.ShapeDtypeStruct((N_TOK, QK_CAT), BF16)],
        compiler_params=_cparams("parallel"),
        name="proj1",
    )(xb, w1, qg, kg, cos64, sin64)


def _glu_body(x_ref, wa_ref, wb_ref, o_ref):
    x = x_ref[...]
    a = _dot(x, wa_ref[...])
    b = _dot(x, wb_ref[...])
    o_ref[...] = a * _sigmoid(b)


def _glu(xb, wa, wb, tm, tn):
    return pl.pallas_call(
        _glu_body,
        grid=(N_TOK // tm, CONV_CH // tn),
        in_specs=[pl.BlockSpec((tm, D_MODEL), lambda i, j: (i, 0)),
                  pl.BlockSpec((D_MODEL, tn), lambda i, j: (0, j)),
                  pl.BlockSpec((D_MODEL, tn), lambda i, j: (0, j))],
        out_specs=pl.BlockSpec((tm, tn), lambda i, j: (i, j)),
        out_shape=jax.ShapeDtypeStruct((N_TOK, CONV_CH), F32),
        compiler_params=_cparams("parallel", "parallel"),
        name="glu",
    )(xb, wa, wb)


CONV_ROWS = 32
CONV_LANES = 256
LN_ROWS = 16


def _conv_taps(win_ref, win_row0, w_ref, b_ref, h_ref, h_row0, rows):
    sub = 8
    base = win_row0 - win_row0 % sub

    def chan_chunk(c, carry):
        c0 = pl.multiple_of(c * CONV_LANES, CONV_LANES)
        cs = pl.ds(c0, CONV_LANES)
        acc = jnp.broadcast_to(b_ref[:, cs], (rows, CONV_LANES))
        lo = win_row0 - base
        span = -(-(lo + CONV_K - 1 + rows) // sub) * sub
        x = win_ref[base:base + span, cs]
        for rho in range(sub):
            offs = [o for o in range(lo, lo + CONV_K) if o % sub == rho]
            xr = x if rho == 0 else pltpu.roll(x, span - rho, 0)
            for o in offs:
                acc = acc + w_ref[o - lo:o - lo + 1, cs] * xr[o - rho:o - rho + rows, :]
        h_ref[h_row0:h_row0 + rows, cs] = acc
        return carry
    lax.fori_loop(0, CONV_CH // CONV_LANES, chan_chunk, 0)


def _ln_swish_rows(h_ref, g_ref, be_ref, o_ref, n_rows):
    def row_chunk(r, carry):
        rs = pl.ds(pl.multiple_of(r * LN_ROWS, LN_ROWS), LN_ROWS)
        y = _layer_norm(h_ref[rs, :], g_ref[...], be_ref[...])
        o_ref[rs, :] = (y * _sigmoid(y)).astype(o_ref.dtype)
        return carry
    lax.fori_loop(0, n_rows // LN_ROWS, row_chunk, 0, unroll=2)


def _conv_p_body(main_ref, prev_ref, w_ref, b_ref, g_ref, be_ref, o_ref, win_sc, h_sc, *, tt):
    first = pl.program_id(1) == 0
    win_sc[0:HALO, :] = jnp.where(first, 0.0, prev_ref[...])
    win_sc[HALO:HALO + tt, :] = main_ref[...]
    lead = HALO - (CONV_K - 1)
    for r0 in range(0, tt, CONV_ROWS):
        _conv_taps(win_sc, r0 + lead, w_ref, b_ref, h_sc, r0, CONV_ROWS)
    _ln_swish_rows(h_sc, g_ref, be_ref, o_ref, tt)


def _conv_prompt(glu, w, b, g, be, tt):
    n_t = SEQ // tt
    const = lambda bi, i: (0, 0)
    return pl.pallas_call(
        functools.partial(_conv_p_body, tt=tt),
        grid=(BATCH, n_t),
        in_specs=[pl.BlockSpec((tt, CONV_CH), lambda bi, i: (bi * n_t + i, 0)),
                  pl.BlockSpec((HALO, CONV_CH),
                               lambda bi, i: (jnp.maximum((bi * SEQ + i * tt) // HALO - 1, 0), 0)),
                  pl.BlockSpec((HALO, CONV_CH), const),
                  pl.BlockSpec((1, CONV_CH), const),
                  pl.BlockSpec((1, CONV_CH), const),
                  pl.BlockSpec((1, CONV_CH), const)],
        out_specs=pl.BlockSpec((tt, CONV_CH), lambda bi, i: (bi * n_t + i, 0)),
        out_shape=jax.ShapeDtypeStruct((P_TOK, CONV_CH), BF16),
        scratch_shapes=[pltpu.VMEM((HALO + tt, CONV_CH), F32), pltpu.VMEM((tt, CONV_CH), F32)],
        compiler_params=_cparams("parallel", "parallel"),
        name="conv_prompt",
    )(glu, glu, w, b, g, be)


def _conv_s_body(state_ref, new_ref, w_ref, b_ref, g_ref, be_ref, o_ref, h_sc, *, gb):
    hist = CONV_K - 1

    def chan_chunk(c, carry):
        cs = pl.ds(pl.multiple_of(c * CONV_LANES, CONV_LANES), CONV_LANES)
        bias = jnp.broadcast_to(b_ref[:, cs], (gb, CONV_LANES))
        acc = [bias] * DEC_SEQ
        for j in range(hist + DEC_SEQ):
            row = state_ref[j, :, cs] if j < hist else new_ref[:, j - hist, cs]
            for t in range(DEC_SEQ):
                k = j - t
                if 0 <= k < CONV_K:
                    acc[t] = acc[t] + w_ref[k:k + 1, cs] * row
        for t in range(DEC_SEQ):
            h_sc[:, t, cs] = acc[t]
        return carry

    lax.fori_loop(0, CONV_CH // CONV_LANES, chan_chunk, 0)
    pair = LN_ROWS // DEC_SEQ
    for r in range(gb // pair):
        h = h_sc[r * pair:(r + 1) * pair].reshape(LN_ROWS, CONV_CH)
        y = _layer_norm(h, g_ref[...], be_ref[...])
        o_ref[r * LN_ROWS:(r + 1) * LN_ROWS, :] = (y * _sigmoid(y)).astype(o_ref.dtype)


def _conv_sample(state_t, glu3, w, b, g, be, gb, layer):
    const = lambda i: (0, 0)
    return pl.pallas_call(
        functools.partial(_conv_s_body, gb=gb),
        grid=(DEC_BATCH // gb,),
        in_specs=[pl.BlockSpec((None, CONV_K - 1, gb, CONV_CH), lambda i: (layer, 0, i, 0)),
                  pl.BlockSpec((gb, DEC_SEQ, CONV_CH), lambda i: (i, 0, 0)),
                  pl.BlockSpec((HALO, CONV_CH), const),
                  pl.BlockSpec((1, CONV_CH), const),
                  pl.BlockSpec((1, CONV_CH), const),
                  pl.BlockSpec((1, CONV_CH), const)],
        out_specs=pl.BlockSpec((gb * DEC_SEQ, CONV_CH), lambda i: (i, 0)),
        out_shape=jax.ShapeDtypeStruct((S_TOK, CONV_CH), BF16),
        scratch_shapes=[pltpu.VMEM((gb, DEC_SEQ, CONV_CH), F32)],
        compiler_params=_cparams("parallel"),
        name="conv_sample",
    )(state_t, glu3, w, b, g, be)


def _q_body(qn_ref, wuq_ref, wuk_ref, cos_ref, sin_ref, *out_refs, sample):
    tm = qn_ref.shape[0]
    q = _dot(qn_ref[...], wuq_ref[...])
    nope_w = N_HEADS * QK_NOPE
    rope_w = N_HEADS * QK_ROPE
    qr = q[:, nope_w:]
    half = QK_ROPE // 2
    lane = lax.broadcasted_iota(jnp.int32, qr.shape, 1)
    ahead = pltpu.roll(qr, rope_w - half, 1)
    behind = pltpu.roll(qr, half, 1)
    swapped = jnp.where((lane % QK_ROPE) < half, ahead, behind)
    rot = qr * cos_ref[...] + swapped * sin_ref[...]
    for h in range(N_HEADS):
        nope = q[:, h * QK_NOPE:(h + 1) * QK_NOPE].astype(BF16)
        lat = _dot(nope, wuk_ref[h]) * MLA_SCALE
        rot_h = rot[:, h * QK_ROPE:(h + 1) * QK_ROPE] * MLA_SCALE
        if sample:
            ql_ref, qr_ref = out_refs
            ql_ref[:, h, :, :] = lat.reshape(tm // DEC_SEQ, DEC_SEQ, KV_LORA)
            qr_ref[:, h, :, :] = rot_h.reshape(tm // DEC_SEQ, DEC_SEQ, QK_ROPE)
        else:
            (qcat_ref,) = out_refs
            qcat_ref[0, h, :, :KV_LORA] = lat.astype(BF16)
            qcat_ref[0, h, :, KV_LORA:KV_LORA + QK_ROPE] = rot_h.astype(BF16)
            qcat_ref[0, h, :, KV_LORA + QK_ROPE:] = jnp.zeros((tm, QK_CAT - KV_LORA - QK_ROPE), BF16)


def _q_prompt(qn, wuq, wuk, cos_q, sin_q, tm):
    n_t = SEQ // tm
    c2 = lambda b, i: (0, 0)
    c3 = lambda b, i: (0, 0, 0)
    return pl.pallas_call(
        functools.partial(_q_body, sample=False),
        grid=(BATCH, n_t),
        in_specs=[pl.BlockSpec((tm, Q_LORA), lambda b, i: (b * n_t + i, 0)),
                  pl.BlockSpec((Q_LORA, N_HEADS * (QK_NOPE + QK_ROPE)), c2),
                  pl.BlockSpec((N_HEADS, QK_NOPE, KV_LORA), c3),
                  pl.BlockSpec((tm, N_HEADS * QK_ROPE), lambda b, i: (i, 0)),
                  pl.BlockSpec((tm, N_HEADS * QK_ROPE), lambda b, i: (i, 0))],
        out_specs=[pl.BlockSpec((1, N_HEADS, tm, QK_CAT), lambda b, i: (b, 0, i, 0))],
        out_shape=[jax.ShapeDtypeStruct((BATCH, N_HEADS, SEQ, QK_CAT), BF16)],
        compiler_params=_cparams("parallel", "parallel"),
        name="q_prompt",
    )(qn, wuq, wuk, cos_q, sin_q)[0]


def _q_sample(qn, wuq, wuk, cos_q, sin_q, tm):
    gb = tm // DEC_SEQ
    pos_blk = SEQ // tm
    c2 = lambda i: (0, 0)
    c3 = lambda i: (0, 0, 0)
    return pl.pallas_call(
        functools.partial(_q_body, sample=True),
        grid=(S_TOK // tm,),
        in_specs=[pl.BlockSpec((tm, Q_LORA), lambda i: (P_TOK // tm + i, 0)),
                  pl.BlockSpec((Q_LORA, N_HEADS * (QK_NOPE + QK_ROPE)), c2),
                  pl.BlockSpec((N_HEADS, QK_NOPE, KV_LORA), c3),
                  pl.BlockSpec((tm, N_HEADS * QK_ROPE), lambda i: (pos_blk, 0)),
                  pl.BlockSpec((tm, N_HEADS * QK_ROPE), lambda i: (pos_blk, 0))],
        out_specs=[pl.BlockSpec((gb, N_HEADS, DEC_SEQ, KV_LORA), lambda i: (i, 0, 0, 0)),
                   pl.BlockSpec((gb, N_HEADS, DEC_SEQ, QK_ROPE), lambda i: (i, 0, 0, 0))],
        out_shape=[jax.ShapeDtypeStruct((DEC_BATCH, N_HEADS, DEC_SEQ, KV_LORA), F32),
                   jax.ShapeDtypeStruct((DEC_BATCH, N_HEADS, DEC_SEQ, QK_ROPE), F32)],
        compiler_params=_cparams("parallel"),
        name="q_sample",
    )(qn, wuq, wuk, cos_q, sin_q)


def _attn_p_body(q_ref, k_ref, wuv_ref, o_ref, m_sc, l_sc, acc_sc, *, tq, tk):
    qi = pl.program_id(1)
    rows = N_HEADS * tq
    q = q_ref[0].reshape(rows, QK_CAT)
    m_sc[...] = jnp.full(m_sc.shape, NEG_INF, F32)
    l_sc[...] = jnp.zeros(l_sc.shape, F32)
    acc_sc[...] = jnp.zeros(acc_sc.shape, F32)
    n_full = (qi * tq) // tk

    def chunk(j, masked):
        ks = pl.ds(pl.multiple_of(j * tk, tk), tk)
        k = k_ref[ks, :]
        s = _dot_nt(q, k)
        if masked:
            qpos = qi * tq + lax.broadcasted_iota(jnp.int32, (tq, tk), 0)
            kpos = j * tk + lax.broadcasted_iota(jnp.int32, (tq, tk), 1)
            s = jnp.where((kpos <= qpos)[None], s.reshape(N_HEADS, tq, tk), NEG_INF).reshape(rows, tk)
        m_prev = m_sc[...]
        m_new = jnp.maximum(m_prev, jnp.max(s, axis=-1, keepdims=True))
        alpha = jnp.exp(m_prev - m_new)
        p = jnp.exp(s - m_new)
        l_sc[...] = alpha * l_sc[...] + jnp.sum(p, axis=-1, keepdims=True)
        acc_sc[...] = alpha * acc_sc[...] + _dot(p.astype(BF16), k[:, :KV_LORA])
        m_sc[...] = m_new

    def full_chunk(j, carry):
        chunk(j, False)
        return carry

    lax.fori_loop(0, n_full, full_chunk, 0)
    chunk(n_full, True)
    lat = acc_sc[...] / l_sc[...]
    for h in range(N_HEADS):
        lat_h = lat[h * tq:(h + 1) * tq, :].astype(BF16)
        o_ref[:, h * V_HEAD:(h + 1) * V_HEAD] = _dot(lat_h, wuv_ref[h]).astype(BF16)


def _attn_prompt(qcat, kcat, wuv, tq, tk):
    n_q = SEQ // tq
    rows = N_HEADS * tq
    return pl.pallas_call(
        functools.partial(_attn_p_body, tq=tq, tk=tk),
        grid=(BATCH, n_q),
        in_specs=[pl.BlockSpec((1, N_HEADS, tq, QK_CAT), lambda b, i: (b, 0, i, 0)),
                  pl.BlockSpec((SEQ, QK_CAT), lambda b, i: (b, 0)),
                  pl.BlockSpec((N_HEADS, KV_LORA, V_HEAD), lambda b, i: (0, 0, 0))],
        out_specs=pl.BlockSpec((tq, N_HEADS * V_HEAD), lambda b, i: (b * n_q + i, 0)),
        out_shape=jax.ShapeDtypeStruct((P_TOK, N_HEADS * V_HEAD), BF16),
        scratch_shapes=[pltpu.VMEM((rows, 1), F32), pltpu.VMEM((rows, 1), F32),
                        pltpu.VMEM((rows, KV_LORA), F32)],
        compiler_params=_cparams("parallel", "arbitrary"),
        name="attn_prompt",
    )(qcat, kcat, wuv)


def _attn_s_body(pt_ref, ql_ref, qr_ref, cn_ref, kn_ref, ckv_hbm, krt_hbm, o_ref,
                 m_sc, l_sc, acc_sc, kb_sc, krt_sc, ck_buf, kr_buf, sem, *, pages, nb, n_pages, layer):
    b = pl.program_id(0)
    g = pl.program_id(1)
    n_g = pl.num_programs(1)
    rows = N_HEADS * DEC_SEQ
    n_in = nb * pages
    step = b * n_g + g
    slot = step % 2

    def page_copies(step_b, step_g, buf_slot):
        copies = []
        for a in range(nb):
            for u in range(pages):
                i = a * pages + u
                page = pt_ref[(step_b * nb + a) * n_pages + step_g * pages + u]
                copies.append(pltpu.make_async_copy(ckv_hbm.at[layer, page], ck_buf.at[buf_slot, i],
                                                    sem.at[buf_slot, i]))
                copies.append(pltpu.make_async_copy(krt_hbm.at[layer, page], kr_buf.at[buf_slot, i],
                                                    sem.at[buf_slot, n_in + i]))
        return copies

    def start_all(copies):
        for n, cp in enumerate(copies):
            cp.start(priority=(n // 2) % 2)

    @pl.when(step == 0)
    def _():
        start_all(page_copies(b, g, slot))

    @pl.when(step + 1 < pl.num_programs(0) * n_g)
    def _():
        nxt = step + 1
        start_all(page_copies(nxt // n_g, nxt % n_g, 1 - slot))

    @pl.when(g == 0)
    def _():
        for a in range(nb):
            ql = ql_ref[a].reshape(rows, KV_LORA)
            qr = qr_ref[a].reshape(rows, QK_ROPE)
            cn = cn_ref[a]
            kn = kn_ref[a]
            s = _dot_nt(ql, cn) + _dot_nt(qr, kn)
            t = lax.broadcasted_iota(jnp.int32, s.shape, 0) % DEC_SEQ
            c = lax.broadcasted_iota(jnp.int32, s.shape, 1)
            s = jnp.where(c <= t, s, NEG_INF)
            m = jnp.max(s, axis=-1, keepdims=True)
            p = jnp.exp(s - m)
            m_sc[a] = m
            l_sc[a] = jnp.sum(p, axis=-1, keepdims=True)
            acc_sc[a] = _dot(p, cn)

    for cp in page_copies(b, g, slot):
        cp.wait()
    for a in range(nb):
        for u in range(pages):
            i = a * pages + u
            kb_sc[a, u * PAGE_SIZE:(u + 1) * PAGE_SIZE, :] = ck_buf[slot, i].astype(BF16)
            krt_sc[a, :, u * PAGE_SIZE:(u + 1) * PAGE_SIZE] = kr_buf[slot, i].astype(BF16)
    for a in range(nb):
        ql = ql_ref[a].reshape(rows, KV_LORA).astype(BF16)
        qr = qr_ref[a].reshape(rows, QK_ROPE).astype(BF16)
        kb = kb_sc[a]
        s = _dot_nt(ql, kb) + _dot(qr, krt_sc[a])
        m_prev = m_sc[a]
        m_new = jnp.maximum(m_prev, jnp.max(s, axis=-1, keepdims=True))
        alpha = jnp.exp(m_prev - m_new)
        p = jnp.exp(s - m_new)
        l_sc[a] = alpha * l_sc[a] + jnp.sum(p, axis=-1, keepdims=True)
        acc_sc[a] = alpha * acc_sc[a] + _dot(p.astype(BF16), kb)
        m_sc[a] = m_new

    @pl.when(g == pl.num_programs(1) - 1)
    def _():
        for a in range(nb):
            o_ref[a] = (acc_sc[a] / l_sc[a]).reshape(N_HEADS, DEC_SEQ, KV_LORA)


def _attn_sample(page_table, ql, qr, ckv_new, kr_new, cache_ckv, cache_krope_t, layer, pages, nb):
    n_pages = page_table.shape[1]
    n_g = n_pages // pages
    rows = N_HEADS * DEC_SEQ
    n_in = nb * pages
    per_b4 = lambda b, g, pt: (b, 0, 0, 0)
    per_b3 = lambda b, g, pt: (b, 0, 0)
    grid_spec = pltpu.PrefetchScalarGridSpec(
        num_scalar_prefetch=1,
        grid=(DEC_BATCH // nb, n_g),
        in_specs=[pl.BlockSpec((nb, N_HEADS, DEC_SEQ, KV_LORA), per_b4),
                  pl.BlockSpec((nb, N_HEADS, DEC_SEQ, QK_ROPE), per_b4),
                  pl.BlockSpec((nb, DEC_SEQ, KV_LORA), per_b3),
                  pl.BlockSpec((nb, DEC_SEQ, QK_ROPE), per_b3),
                  pl.BlockSpec(memory_space=pl.ANY),
                  pl.BlockSpec(memory_space=pl.ANY)],
        out_specs=pl.BlockSpec((nb, N_HEADS, DEC_SEQ, KV_LORA), per_b4),
        scratch_shapes=[pltpu.VMEM((nb, rows, 1), F32), pltpu.VMEM((nb, rows, 1), F32),
                        pltpu.VMEM((nb, rows, KV_LORA), F32),
                        pltpu.VMEM((nb, pages * PAGE_SIZE, KV_LORA), BF16),
                        pltpu.VMEM((nb, QK_ROPE, pages * PAGE_SIZE), BF16),
                        pltpu.VMEM((2, n_in, PAGE_SIZE, KV_LORA), F32),
                        pltpu.VMEM((2, n_in, QK_ROPE, PAGE_SIZE), F32),
                        pltpu.SemaphoreType.DMA((2, 2 * n_in))])
    return pl.pallas_call(
        functools.partial(_attn_s_body, pages=pages, nb=nb, n_pages=n_pages, layer=layer),
        grid_spec=grid_spec,
        out_shape=jax.ShapeDtypeStruct((DEC_BATCH, N_HEADS, DEC_SEQ, KV_LORA), F32),
        compiler_params=_cparams("arbitrary", "arbitrary"),
        name="attn_sample",
    )(page_table.reshape(-1), ql, qr, ckv_new, kr_new, cache_ckv, cache_krope_t)


def _vup_s_body(lat_ref, wuv_ref, o_ref):
    lat = lat_ref[...].reshape(S_TOK, KV_LORA).astype(BF16)
    o_ref[...] = _dot(lat, wuv_ref[0]).astype(BF16)


def _vup_sample(lat, wuv):
    return pl.pallas_call(
        _vup_s_body,
        grid=(N_HEADS,),
        in_specs=[pl.BlockSpec((DEC_BATCH, None, DEC_SEQ, KV_LORA), lambda h: (0, h, 0, 0)),
                  pl.BlockSpec((1, KV_LORA, V_HEAD), lambda h: (h, 0, 0))],
        out_specs=pl.BlockSpec((S_TOK, V_HEAD), lambda h: (0, h)),
        out_shape=jax.ShapeDtypeStruct((S_TOK, N_HEADS * V_HEAD), BF16),
        compiler_params=_cparams("parallel"),
        name="vup_sample",
    )(lat, wuv)


def _merge_body(vp_ref, vs_ref, hp_ref, hs_ref, x_ref, wo_ref, wp_ref, wga_ref, wgb_ref, ba_ref, bb_ref,
                o_ref, v_sc, h_sc, *, n_p):
    i = pl.program_id(0)
    j = pl.program_id(1)

    @pl.when((j == 0) & (i < n_p))
    def _():
        v_sc[...] = vp_ref[...]
        h_sc[...] = hp_ref[...]

    @pl.when((j == 0) & (i >= n_p))
    def _():
        v_sc[...] = vs_ref[...]
        h_sc[...] = hs_ref[...]

    x = x_ref[...]
    branch_a = _dot(v_sc[...], wo_ref[...])
    branch_b = _dot(h_sc[...], wp_ref[...])
    gate_a = _sigmoid(_dot(x, wga_ref[...]) + ba_ref[...])
    gate_b = _sigmoid(_dot(x, wgb_ref[...]) + bb_ref[...])
    o_ref[...] = (gate_a * branch_a + gate_b * branch_b).astype(BF16)


def _merge(v_p, v_s, hb_p, hb_s, xb, wo, wp, wg, bg, tm, tn):
    n_j = D_MODEL // tn
    n_p = P_TOK // tm
    row = lambda i, j: (i, 0)
    row_p = lambda i, j: (jnp.minimum(i, n_p - 1), 0)
    row_s = lambda i, j: (jnp.maximum(i - n_p, 0), 0)
    col = lambda i, j: (0, j)
    col_b = lambda i, j: (0, j + n_j)
    return pl.pallas_call(
        functools.partial(_merge_body, n_p=n_p),
        grid=(N_TOK // tm, n_j),
        in_specs=[pl.BlockSpec((tm, D_MODEL), row_p),
                  pl.BlockSpec((tm, D_MODEL), row_s),
                  pl.BlockSpec((tm, D_MODEL), row_p),
                  pl.BlockSpec((tm, D_MODEL), row_s),
                  pl.BlockSpec((tm, D_MODEL), row),
                  pl.BlockSpec((D_MODEL, tn), col),
                  pl.BlockSpec((D_MODEL, tn), col),
                  pl.BlockSpec((D_MODEL, tn), col),
                  pl.BlockSpec((D_MODEL, tn), col_b),
                  pl.BlockSpec((1, tn), col),
                  pl.BlockSpec((1, tn), col_b)],
        out_specs=pl.BlockSpec((tm, tn), lambda i, j: (i, j)),
        out_shape=jax.ShapeDtypeStruct((N_TOK, D_MODEL), BF16),
        scratch_shapes=[pltpu.VMEM((tm, D_MODEL), BF16), pltpu.VMEM((tm, D_MODEL), BF16)],
        compiler_params=_cparams("parallel", "arbitrary"),
        name="merge",
    )(v_p, v_s, hb_p, hb_s, xb, wo, wp, wg, wg, bg, bg)


def _mm_res_ln_body(a_ref, w_ref, res_ref, g_ref, b_ref, o_ref, ob_ref):
    z = DEEPNORM_ALPHA * res_ref[...] + _dot(a_ref[...], w_ref[...])
    y = _layer_norm(z, g_ref[...], b_ref[...])
    o_ref[...] = y
    ob_ref[...] = y.astype(BF16)


def _mm_res_ln(a, w, res, g, b, tm):
    k = a.shape[1]
    row = lambda i: (i, 0)
    const = lambda i: (0, 0)
    return pl.pallas_call(
        _mm_res_ln_body,
        grid=(N_TOK // tm,),
        in_specs=[pl.BlockSpec((tm, k), row),
                  pl.BlockSpec((k, D_MODEL), const),
                  pl.BlockSpec((tm, D_MODEL), row),
                  pl.BlockSpec((1, D_MODEL), const),
                  pl.BlockSpec((1, D_MODEL), const)],
        out_specs=[pl.BlockSpec((tm, D_MODEL), row), pl.BlockSpec((tm, D_MODEL), row)],
        out_shape=[jax.ShapeDtypeStruct((N_TOK, D_MODEL), F32),
                   jax.ShapeDtypeStruct((N_TOK, D_MODEL), BF16)],
        compiler_params=_cparams("parallel"),
        name="mm_res_ln",
    )(a, w, res, g, b)


def _xattn_tail(o, x_ref, wo_ref, g_ref, b_ref, wr_ref, br_ref, o_ref, ob_ref, lg_ref):
    z = DEEPNORM_ALPHA * x_ref[...] + _dot(o, wo_ref[...])
    y = _layer_norm(z, g_ref[...], b_ref[...])
    o_ref[...] = y
    ob_ref[...] = y.astype(BF16)
    lg_ref[...] = _dot(y.astype(BF16), wr_ref[...]) + br_ref[...]


def _xattn_p_body(x_ref, xb_ref, wq_ref, mk_ref, mv_ref, wo_ref, g_ref, b_ref, wr_ref, br_ref,
                  o_ref, ob_ref, lg_ref):
    q = _dot(xb_ref[...], wq_ref[...])
    heads = []
    for h in range(XA_HEADS):
        hs = slice(h * XA_HEAD_DIM, (h + 1) * XA_HEAD_DIM)
        kh = mk_ref[0, :, hs].astype(BF16)
        vh = mv_ref[0, :, hs].astype(BF16)
        s = _dot_nt(q[:, hs].astype(BF16), kh) * XA_SCALE
        e = jnp.exp(s - jnp.max(s, axis=-1, keepdims=True))
        p = e / jnp.sum(e, axis=-1, keepdims=True)
        heads.append(_dot(p.astype(BF16), vh))
    o = jnp.concatenate(heads, axis=-1).astype(BF16)
    _xattn_tail(o, x_ref, wo_ref, g_ref, b_ref, wr_ref, br_ref, o_ref, ob_ref, lg_ref)


def _xattn_s_body(x_ref, xb_ref, wq_ref, mk_ref, mv_ref, wo_ref, g_ref, b_ref, wr_ref, br_ref,
                  o_ref, ob_ref, lg_ref, *, gb):
    q = _dot(xb_ref[...], wq_ref[...])
    rows = XA_HEADS * DEC_SEQ
    keys = N_MEM * XA_HEADS
    row_head = lax.broadcasted_iota(jnp.int32, (rows, keys), 0) // DEC_SEQ
    key_head = lax.broadcasted_iota(jnp.int32, (rows, keys), 1) % XA_HEADS
    own = row_head == key_head
    outs = []
    for gi in range(gb):
        qg = q[gi * DEC_SEQ:(gi + 1) * DEC_SEQ, :]
        q4 = jnp.concatenate([qg[:, h * XA_HEAD_DIM:(h + 1) * XA_HEAD_DIM] for h in range(XA_HEADS)],
                             axis=0).astype(BF16)
        k = mk_ref[gi].astype(BF16)
        v = mv_ref[gi].astype(BF16)
        s = jnp.where(own, _dot_nt(q4, k) * XA_SCALE, NEG_INF)
        e = jnp.exp(s - jnp.max(s, axis=-1, keepdims=True))
        p = e / jnp.sum(e, axis=-1, keepdims=True)
        o4 = _dot(p.astype(BF16), v)
        outs.append(jnp.concatenate([o4[h * DEC_SEQ:(h + 1) * DEC_SEQ, :] for h in range(XA_HEADS)], axis=1))
    o = jnp.concatenate(outs, axis=0).astype(BF16)
    _xattn_tail(o, x_ref, wo_ref, g_ref, b_ref, wr_ref, br_ref, o_ref, ob_ref, lg_ref)


def _xattn(body, x, xb, wq, mk, mv, wo, g, b, wr, br, *, row0, n_rows, tm, mem_spec, name):
    blk0 = row0 // tm
    hd = XA_HEADS * XA_HEAD_DIM
    row = lambda i: (blk0 + i, 0)
    out_row = lambda i: (i, 0)
    const = lambda i: (0, 0)
    return pl.pallas_call(
        body,
        grid=(n_rows // tm,),
        in_specs=[pl.BlockSpec((tm, D_MODEL), row),
                  pl.BlockSpec((tm, D_MODEL), row),
                  pl.BlockSpec((D_MODEL, hd), const),
                  mem_spec,
                  mem_spec,
                  pl.BlockSpec((hd, D_MODEL), const),
                  pl.BlockSpec((1, D_MODEL), const),
                  pl.BlockSpec((1, D_MODEL), const),
                  pl.BlockSpec((D_MODEL, ROUTER_LANES), const),
                  pl.BlockSpec((1, ROUTER_LANES), const)],
        out_specs=[pl.BlockSpec((tm, D_MODEL), out_row), pl.BlockSpec((tm, D_MODEL), out_row),
                   pl.BlockSpec((tm, ROUTER_LANES), out_row)],
        out_shape=[jax.ShapeDtypeStruct((n_rows, D_MODEL), F32),
                   jax.ShapeDtypeStruct((n_rows, D_MODEL), BF16),
                   jax.ShapeDtypeStruct((n_rows, ROUTER_LANES), F32)],
        compiler_params=_cparams("parallel"),
        name=name,
    )(x, xb, wq, mk, mv, wo, g, b, wr, br)


def _route(logits):
    lane = lax.broadcasted_iota(jnp.int32, logits.shape, 1)
    big = jnp.int32(ROUTER_LANES)
    gl = jnp.where(lane < N_GROUPS, logits, NEG_INF)
    gmax = jnp.max(gl, axis=-1, keepdims=True)
    g_sel = jnp.min(jnp.where(gl == gmax, lane, big), axis=-1, keepdims=True)
    g_w = 1.0 / jnp.sum(jnp.exp(gl - gmax), axis=-1, keepdims=True)
    lo = N_GROUPS + EXP_PER_GROUP * g_sel
    el = jnp.where((lane >= lo) & (lane < lo + EXP_PER_GROUP), logits, NEG_INF)
    v1 = jnp.max(el, axis=-1, keepdims=True)
    i1 = jnp.min(jnp.where(el == v1, lane, big), axis=-1, keepdims=True)
    el2 = jnp.where(lane == i1, NEG_INF, el)
    v2 = jnp.max(el2, axis=-1, keepdims=True)
    i2 = jnp.min(jnp.where(el2 == v2, lane, big), axis=-1, keepdims=True)
    e2 = jnp.exp(v2 - v1)
    w1 = g_w / (1.0 + e2)
    w2 = g_w * e2 / (1.0 + e2)
    return jnp.where(lane == i1, w1, 0.0) + jnp.where(lane == i2, w2, 0.0)


MOE_TILE = 768
MOE_CHUNK = 128


def _moe_body(lg_ref, xb_ref, tri_ref, wg_ref, wu_ref, wd_ref, o_ref, comb_sc, combt_sc, rank_sc, rankt_sc):
    e = pl.program_id(1)
    tm = xb_ref.shape[0]

    @pl.when(e == 0)
    def _():
        comb = _route(lg_ref[...])
        combt = comb.T
        comb_sc[...] = comb
        combt_sc[...] = combt
        chosen = jnp.where(combt > 0.0, 1.0, 0.0).astype(BF16)
        rank = _dot(chosen, tri_ref[...])
        rank_sc[...] = rank
        rankt_sc[...] = rank.T
        o_ref[...] = jnp.zeros(o_ref.shape, F32)

    le = N_GROUPS + e
    w_row = combt_sc[pl.ds(le, 1), :]
    r_row = rank_sc[pl.ds(le, 1), :]
    lane = lax.broadcasted_iota(jnp.int32, (tm, ROUTER_LANES), 1)
    w_col = jnp.sum(jnp.where(lane == le, comb_sc[...], 0.0), axis=-1, keepdims=True)
    r_col = jnp.sum(jnp.where(lane == le, rankt_sc[...], 0.0), axis=-1, keepdims=True)
    count = jnp.sum(jnp.where(w_row > 0.0, 1.0, 0.0)).astype(jnp.int32)
    slot_r = lax.broadcasted_iota(jnp.int32, (MOE_CHUNK, tm), 0).astype(F32)
    slot_c = lax.broadcasted_iota(jnp.int32, (tm, MOE_CHUNK), 1).astype(F32)

    def chunk(ci, carry):
        off = (ci * MOE_CHUNK).astype(F32)
        hit = (r_row == slot_r + off) & (w_row > 0.0)
        xg = _dot(jnp.where(hit, 1.0, 0.0).astype(BF16), xb_ref[...]).astype(BF16)
        cg = jnp.sum(jnp.where(hit, w_row, 0.0), axis=-1, keepdims=True)
        gate = _dot(xg, wg_ref[...])
        up = _dot(xg, wu_ref[...])
        hid = (gate * _sigmoid(gate)) * up * cg
        y = _dot(hid.astype(BF16), wd_ref[...]).astype(BF16)
        hit_t = (r_col == slot_c + off) & (w_col > 0.0)
        o_ref[...] += _dot(jnp.where(hit_t, 1.0, 0.0).astype(BF16), y)
        return carry

    lax.fori_loop(0, (count + MOE_CHUNK - 1) // MOE_CHUNK, chunk, 0)


def _moe(logits, xb, tri, wg, wu, wd, layer):
    tm = MOE_TILE
    row = lambda i, e: (i, 0)
    const = lambda i, e: (0, 0)
    expert = lambda i, e: (layer, e, 0, 0)
    return pl.pallas_call(
        _moe_body,
        grid=(N_TOK // tm, N_EXPERTS),
        in_specs=[pl.BlockSpec((tm, ROUTER_LANES), row),
                  pl.BlockSpec((tm, D_MODEL), row),
                  pl.BlockSpec((tm, tm), const),
                  pl.BlockSpec((None, None, D_MODEL, D_EXPERT), expert),
                  pl.BlockSpec((None, None, D_MODEL, D_EXPERT), expert),
                  pl.BlockSpec((None, None, D_EXPERT, D_MODEL), expert)],
        out_specs=pl.BlockSpec((tm, D_MODEL), row),
        out_shape=jax.ShapeDtypeStruct((N_TOK, D_MODEL), F32),
        scratch_shapes=[pltpu.VMEM((tm, ROUTER_LANES), F32), pltpu.VMEM((ROUTER_LANES, tm), F32),
                        pltpu.VMEM((ROUTER_LANES, tm), F32), pltpu.VMEM((tm, ROUTER_LANES), F32)],
        compiler_params=_cparams("parallel", "arbitrary"),
        name="moe",
    )(logits, xb, tri, wg, wu, wd)


def _add_ln_body(xp_ref, xs_ref, acc_ref, g_ref, b_ref, o_ref, ob_ref, *, n_p):
    x = jnp.where(pl.program_id(0) < n_p, xp_ref[...], xs_ref[...])
    y = _layer_norm(DEEPNORM_ALPHA * x + acc_ref[...], g_ref[...], b_ref[...])
    o_ref[...] = y
    ob_ref[...] = y.astype(BF16)


def _add_ln(x_p, x_s, acc, g, b, tm):
    n_p = P_TOK // tm
    row = lambda i: (i, 0)
    const = lambda i: (0, 0)
    return pl.pallas_call(
        functools.partial(_add_ln_body, n_p=n_p),
        grid=(N_TOK // tm,),
        in_specs=[pl.BlockSpec((tm, D_MODEL), lambda i: (jnp.minimum(i, n_p - 1), 0)),
                  pl.BlockSpec((tm, D_MODEL), lambda i: (jnp.maximum(i - n_p, 0), 0)),
                  pl.BlockSpec((tm, D_MODEL), row),
                  pl.BlockSpec((1, D_MODEL), const),
                  pl.BlockSpec((1, D_MODEL), const)],
        out_specs=[pl.BlockSpec((tm, D_MODEL), row), pl.BlockSpec((tm, D_MODEL), row)],
        out_shape=[jax.ShapeDtypeStruct((N_TOK, D_MODEL), F32),
                   jax.ShapeDtypeStruct((N_TOK, D_MODEL), BF16)],
        compiler_params=_cparams("parallel"),
        name="add_ln",
    )(x_p, x_s, acc, g, b)


def _rotary(pos):
    half = QK_ROPE // 2
    inv = ROPE_THETA ** (-jnp.arange(half, dtype=F32) / half)
    ang = pos.astype(F32)[:, None] * inv[None, :]
    cos, sin = jnp.cos(ang), jnp.sin(ang)
    return jnp.concatenate([cos, cos], axis=-1), jnp.concatenate([-sin, sin], axis=-1)


TM_PROJ = 512
TM_Q = 256
TQ_ATTN = 128
TK_ATTN = 512
TT_CONV = 128
GB_CONV = 8
GB_XA = 8
PAGES_PER_STEP = 16
ELEMS_PER_STEP = 2
TM_LN = 256
TM_ADD_LN = 512


def kernel(x_prompt, x_sample, mem_prompt, cache_ckv, cache_krope, cache_mem_k, cache_mem_v, state_conv, page_table, w_in, b_gate, q_norm_g, w_uq, kv_norm_g, w_uk, w_uv, w_o_mla, conv_w, conv_b, conv_ln_g, conv_ln_b, w_pw2, w_out, xa_wq, xa_wk, xa_wv, xa_wo, router_group_w, router_group_b, router_expert_w, router_expert_b, w_gate_e, w_up_e, w_down_e, ln_g, ln_b):
    n_pages = page_table.shape[1]
    past_len = n_pages * PAGE_SIZE
    hd = XA_HEADS * XA_HEAD_DIM

    pos_rows = max(TM_PROJ, TM_Q)
    pos = jnp.concatenate([jnp.arange(SEQ, dtype=jnp.int32),
                           past_len + (jnp.arange(pos_rows, dtype=jnp.int32) % DEC_SEQ)])
    cos64, sin64 = _rotary(pos)
    cos_q = jnp.tile(cos64, (1, N_HEADS))
    sin_q = jnp.tile(sin64, (1, N_HEADS))
    n_pos_k = SEQ + TM_PROJ
    n_pos_q = SEQ + TM_Q

    x = jnp.concatenate([x_prompt.reshape(P_TOK, D_MODEL), x_sample.reshape(S_TOK, D_MODEL)], axis=0)
    xb = x.astype(BF16)

    w_mem = jnp.concatenate([xa_wk.reshape(DEPTH, D_MODEL, hd), xa_wv.reshape(DEPTH, D_MODEL, hd)], axis=-1)
    w_mem = jnp.moveaxis(w_mem, 0, 1).reshape(D_MODEL, DEPTH * 2 * hd).astype(BF16)
    mem_kv = _mm(mem_prompt.reshape(BATCH * N_MEM, D_MODEL).astype(BF16), w_mem, 512, 512)
    mem_kv = mem_kv.reshape(BATCH, N_MEM, DEPTH, 2, hd)

    cache_krope_t = jnp.swapaxes(cache_krope, 2, 3)
    mem_k_rows = cache_mem_k.reshape(DEPTH, DEC_BATCH, N_MEM * XA_HEADS, XA_HEAD_DIM)
    mem_v_rows = cache_mem_v.reshape(DEPTH, DEC_BATCH, N_MEM * XA_HEADS, XA_HEAD_DIM)
    tri = jnp.triu(jnp.ones((MOE_TILE, MOE_TILE), F32), k=1).astype(BF16)
    state_t = jnp.transpose(state_conv, (0, 2, 1, 3))
    w_gate_b, w_up_b, w_down_b =w_gate_e.astype(BF16), w_up_e.astype(BF16), w_down_e.astype(BF16)

    outs ={k: [] for k in ("ckv_p", "kr_p", "mk_p", "mv_p", "cv_p", "ckv_s", "kr_s", "cv_s")}
    for l in range(DEPTH):
        wl = w_in[l]
        w1 = wl[:, :C1].astype(BF16)
        wa = wl[:, C1:C1 + CONV_CH].astype(BF16)
        wb = wl[:, C1 + CONV_CH:C1 + 2 * CONV_CH].astype(BF16)
        wg = wl[:, C1 + 2 * CONV_CH:].astype(BF16)
        wuq = jnp.concatenate([w_uq[l][:, :, :QK_NOPE].reshape(Q_LORA, N_HEADS * QK_NOPE),
                               w_uq[l][:, :, QK_NOPE:].reshape(Q_LORA, N_HEADS * QK_ROPE)], axis=-1).astype(BF16)
        wuk = jnp.transpose(w_uk[l], (1, 2, 0)).astype(BF16)
        wuv = jnp.transpose(w_uv[l], (1, 0, 2)).astype(BF16)
        wo = w_o_mla[l].reshape(N_HEADS * V_HEAD, D_MODEL).astype(BF16)
        wp = w_pw2[l].astype(BF16)
        wout = w_out[l].astype(BF16)
        wxq = xa_wq[l].reshape(D_MODEL, hd).astype(BF16)
        wxo = xa_wo[l].reshape(hd, D_MODEL).astype(BF16)
        wr = jnp.concatenate([router_group_w[l], router_expert_w[l]], axis=-1)
        wr = jnp.pad(wr, ((0, 0), (0, ROUTER_LANES - wr.shape[1]))).astype(BF16)
        br = jnp.concatenate([router_group_b[l], router_expert_b[l]])
        br = jnp.pad(br, (0, ROUTER_LANES - br.shape[0]))[None, :]
        conv_w_l = jnp.pad(conv_w[l], ((0, HALO - CONV_K), (0, 0)))
        row = lambda v: v[None, :]

        qn, ckv, kr, kcat = _proj1(xb, w1, row(q_norm_g[l]), row(kv_norm_g[l]),
                                        cos64[:n_pos_k], sin64[:n_pos_k], TM_PROJ)
        glu = _glu(xb, wa, wb, 512, 512)

        hb_p = _conv_prompt(glu, conv_w_l, row(conv_b[l]), row(conv_ln_g[l]), row(conv_ln_b[l]), TT_CONV)
        glu_s = glu[P_TOK:].reshape(DEC_BATCH, DEC_SEQ, CONV_CH)
        hb_s = _conv_sample(state_t, glu_s, conv_w_l, row(conv_b[l]), row(conv_ln_g[l]),
                            row(conv_ln_b[l]), GB_CONV, l)

        qcat_p = _q_prompt(qn, wuq, wuk, cos_q[:n_pos_q], sin_q[:n_pos_q], TM_Q)
        ql_s, qr_s = _q_sample(qn, wuq, wuk, cos_q[:n_pos_q], sin_q[:n_pos_q], TM_Q)
        v_p = _attn_prompt(qcat_p, kcat, wuv, TQ_ATTN, TK_ATTN)
        ckv_s = ckv[P_TOK:].reshape(DEC_BATCH, DEC_SEQ, KV_LORA)
        kr_s = kr[P_TOK:].reshape(DEC_BATCH, DEC_SEQ, QK_ROPE)
        lat_s = _attn_sample(page_table, ql_s, qr_s, ckv_s, kr_s, cache_ckv, cache_krope_t, l,
                             PAGES_PER_STEP, ELEMS_PER_STEP)
        v_s = _vup_sample(lat_s, wuv)

        merged = _merge(v_p, v_s, hb_p, hb_s, xb, wo, wp, wg, row(b_gate[l]), 512, 512)
        x1, x1b = _mm_res_ln(merged, wout, x, row(ln_g[l, 0]), row(ln_b[l, 0]), TM_LN)

        mk_p = mem_kv[:, :, l, 0]
        mv_p = mem_kv[:, :, l, 1]
        xa_args = (wxo, row(ln_g[l, 1]), row(ln_b[l, 1]), wr, br)
        x2_p, x2b_p, lg_p = _xattn(
            _xattn_p_body, x1, x1b, wxq, mk_p, mv_p, *xa_args, row0=0, n_rows=P_TOK, tm=512,
            mem_spec=pl.BlockSpec((1, N_MEM, hd), lambda i: (i // (SEQ // 512), 0, 0)), name="xattn_prompt")
        x2_s, x2b_s, lg_s = _xattn(
            functools.partial(_xattn_s_body, gb=GB_XA), x1, x1b, wxq, mem_k_rows, mem_v_rows, *xa_args,
            row0=P_TOK, n_rows=S_TOK, tm=GB_XA * DEC_SEQ,
            mem_spec=pl.BlockSpec((None, GB_XA, N_MEM * XA_HEADS, XA_HEAD_DIM), lambda i, l=l: (l, i, 0, 0)),
            name="xattn_sample")

        moe_out = _moe(jnp.concatenate([lg_p, lg_s], axis=0), jnp.concatenate([x2b_p, x2b_s], axis=0), tri,
                       w_gate_b, w_up_b, w_down_b, l)
        x, xb = _add_ln(x2_p, x2_s, moe_out, row(ln_g[l, 2]), row(ln_b[l, 2]), TM_ADD_LN)

        outs["ckv_p"].append(ckv[:P_TOK].reshape(BATCH, SEQ, KV_LORA))
        outs["kr_p"].append(kr[:P_TOK].reshape(BATCH, SEQ, QK_ROPE))
        outs["mk_p"].append(mk_p.reshape(BATCH, N_MEM, XA_HEADS, XA_HEAD_DIM))
        outs["mv_p"].append(mv_p.reshape(BATCH, N_MEM, XA_HEADS, XA_HEAD_DIM))
        outs["cv_p"].append(jnp.stack([glu[(bi + 1) * SEQ - (CONV_K - 1):(bi + 1) * SEQ] for bi in range(BATCH)]))
        outs["ckv_s"].append(ckv_s)
        outs["kr_s"].append(kr_s)
        outs["cv_s"].append(jnp.concatenate([state_conv[l][:, DEC_SEQ:], glu_s], axis=1))

    return (x[:P_TOK].reshape(BATCH, SEQ, D_MODEL), x[P_TOK:].reshape(DEC_BATCH, DEC_SEQ, D_MODEL),
            jnp.stack(outs["ckv_p"]), jnp.stack(outs["kr_p"]), jnp.stack(outs["mk_p"]), jnp.stack(outs["mv_p"]),
            jnp.stack(outs["cv_p"]), jnp.stack(outs["ckv_s"]), jnp.stack(outs["kr_s"]), jnp.stack(outs["cv_s"]))
```

```python
import functools

import jax
import jax.numpy as jnp
from jax import lax
from jax.experimental import pallas as pl
from jax.experimental.pallas import tpu as pltpu

D_MODEL = 2048
BATCH = 4
SEQ = 2048
DEPTH = 2
DEC_BATCH = 128
DEC_SEQ = 8
PAGE_SIZE = 128
N_HEADS = 16
QK_NOPE = 128
QK_ROPE = 64
V_HEAD = 128
Q_LORA = 512
KV_LORA = 512
ROPE_THETA = 10000.0
CONV_CH = D_MODEL
CONV_K = 31
N_MEM = 256
XA_HEADS = 4
XA_HEAD_DIM = 128
N_GROUPS = 4
EXP_PER_GROUP = 4
N_EXPERTS = 16
D_EXPERT = 512
LN_EPS = 1e-5
RMS_EPS = 1e-6
DEEPNORM_ALPHA = (2 * DEPTH) ** 0.25
NEG_INF = -1e30

P_TOK = BATCH * SEQ
S_TOK = DEC_BATCH * DEC_SEQ
N_TOK = P_TOK + S_TOK
C1 = Q_LORA + KV_LORA + QK_ROPE
MLA_SCALE = (QK_NOPE + QK_ROPE) ** -0.5
XA_SCALE = XA_HEAD_DIM ** -0.5
HALO = 32
ROUTER_LANES = 128
VMEM_LIMIT = 56 * 1024 * 1024

BF16 = jnp.bfloat16
F32 = jnp.float32


def _cparams(*sem):
    return pltpu.CompilerParams(dimension_semantics=sem, vmem_limit_bytes=VMEM_LIMIT)


def _dot(a, b):
    return jnp.dot(a, b, preferred_element_type=F32)


def _dot_nt(a, b):
    return lax.dot_general(a, b, (((1,), (1,)), ((), ())), preferred_element_type=F32)


def _layer_norm(z, g, b):
    mu = jnp.mean(z, axis=-1, keepdims=True)
    zc = z - mu
    var = jnp.mean(zc * zc, axis=-1, keepdims=True)
    return zc * lax.rsqrt(var + LN_EPS) * g + b


def _sigmoid(x):
    return 1.0 / (1.0 + jnp.exp(-x))


def _mm_body(a_ref, b_ref, o_ref):
    o_ref[...] = _dot(a_ref[...], b_ref[...])


def _mm(a, b, tm, tn):
    m, k = a.shape
    n = b.shape[1]
    return pl.pallas_call(
        _mm_body,
        grid=(m // tm, n // tn),
        in_specs=[pl.BlockSpec((tm, k), lambda i, j: (i, 0)),
                  pl.BlockSpec((k, tn), lambda i, j: (0, j))],
        out_specs=pl.BlockSpec((tm, tn), lambda i, j: (i, j)),
        out_shape=jax.ShapeDtypeStruct((m, n), F32),
        compiler_params=_cparams("parallel", "parallel"),
        name="mm",
    )(a, b)


QK_CAT = KV_LORA + 2 * QK_ROPE


def _proj1_body(x_ref, w_ref, qg_ref, kg_ref, cos_ref, sin_ref,
                qn_ref, ckv_ref, kr_ref, kcat_ref):
    acc = _dot(x_ref[...], w_ref[...])
    qc = acc[:, :Q_LORA]
    kvc = acc[:, Q_LORA:Q_LORA + KV_LORA]
    kr = acc[:, Q_LORA + KV_LORA:]
    qn = qc * lax.rsqrt(jnp.mean(qc * qc, axis=-1, keepdims=True) + RMS_EPS) * qg_ref[...]
    qn_ref[...] = qn.astype(BF16)
    ckv = kvc * lax.rsqrt(jnp.mean(kvc * kvc, axis=-1, keepdims=True) + RMS_EPS) * kg_ref[...]
    ckv_ref[...] = ckv
    half = QK_ROPE // 2
    swapped = jnp.concatenate([kr[:, half:], kr[:, :half]], axis=-1)
    rot = kr * cos_ref[...] + swapped * sin_ref[...]
    kr_ref[...] = rot
    kcat_ref[:, :KV_LORA] = ckv.astype(BF16)
    kcat_ref[:, KV_LORA:KV_LORA + QK_ROPE] = rot.astype(BF16)
    kcat_ref[:, KV_LORA + QK_ROPE:] = jnp.zeros((rot.shape[0], QK_CAT - KV_LORA - QK_ROPE), BF16)


def _pos_block_map(tm):
    n_p = P_TOK // tm
    per_seq = SEQ // tm
    return lambda i: (jnp.where(i < n_p, i % per_seq, per_seq), 0)


def _proj1(xb, w1, qg, kg, cos64, sin64, tm):
    row = lambda i: (i, 0)
    const = lambda i: (0, 0)
    return pl.pallas_call(
        _proj1_body,
        grid=(N_TOK // tm,),
        in_specs=[pl.BlockSpec((tm, D_MODEL), row),
                  pl.BlockSpec((D_MODEL, C1), const),
                  pl.BlockSpec((1, Q_LORA), const),
                  pl.BlockSpec((1, KV_LORA), const),
                  pl.BlockSpec((tm, QK_ROPE), _pos_block_map(tm)),
                  pl.BlockSpec((tm, QK_ROPE), _pos_block_map(tm))],
        out_specs=[pl.BlockSpec((tm, Q_LORA), row),
                   pl.BlockSpec((tm, KV_LORA), row),
                   pl.BlockSpec((tm, QK_ROPE), row),
                   pl.BlockSpec((tm, QK_CAT), row)],
        out_shape=[jax.ShapeDtypeStruct((N_TOK, Q_LORA), BF16),
                   jax.ShapeDtypeStruct((N_TOK, KV_LORA), F32),
                   jax.ShapeDtypeStruct((N_TOK, QK_ROPE), F32),
                   jax.ShapeDtypeStruct((N_TOK, QK_CAT), BF16)],
        compiler_params=_cparams("parallel"),
        name="proj1",
    )(xb, w1, qg, kg, cos64, sin64)


def _glu_body(x_ref, wa_ref, wb_ref, o_ref):
    x = x_ref[...]
    a = _dot(x, wa_ref[...])
    b = _dot(x, wb_ref[...])
    o_ref[...] = a * _sigmoid(b)


def _glu(xb, wa, wb, tm, tn):
    return pl.pallas_call(
        _glu_body,
        grid=(N_TOK // tm, CONV_CH // tn),
        in_specs=[pl.BlockSpec((tm, D_MODEL), lambda i, j: (i, 0)),
                  pl.BlockSpec((D_MODEL, tn), lambda i, j: (0, j)),
                  pl.BlockSpec((D_MODEL, tn), lambda i, j: (0, j))],
        out_specs=pl.BlockSpec((tm, tn), lambda i, j: (i, j)),
        out_shape=jax.ShapeDtypeStruct((N_TOK, CONV_CH), F32),
        compiler_params=_cparams("parallel", "parallel"),
        name="glu",
    )(xb, wa, wb)


CONV_ROWS = 32
CONV_LANES = 256
LN_ROWS = 16


def _conv_taps(win_ref, win_row0, w_ref, b_ref, h_ref, h_row0, rows):
    sub = 8
    base = win_row0 - win_row0 % sub

    def chan_chunk(c, carry):
        c0 = pl.multiple_of(c * CONV_LANES, CONV_LANES)
        cs = pl.ds(c0, CONV_LANES)
        acc = jnp.broadcast_to(b_ref[:, cs], (rows, CONV_LANES))
        lo = win_row0 - base
        span = -(-(lo + CONV_K - 1 + rows) // sub) * sub
        x = win_ref[base:base + span, cs]
        for rho in range(sub):
            offs = [o for o in range(lo, lo + CONV_K) if o % sub == rho]
            xr = x if rho == 0 else pltpu.roll(x, span - rho, 0)
            for o in offs:
                acc = acc + w_ref[o - lo:o - lo + 1, cs] * xr[o - rho:o - rho + rows, :]
        h_ref[h_row0:h_row0 + rows, cs] = acc
        return carry
    lax.fori_loop(0, CONV_CH // CONV_LANES, chan_chunk, 0)


def _ln_swish_rows(h_ref, g_ref, be_ref, o_ref, n_rows):
    def row_chunk(r, carry):
        rs = pl.ds(pl.multiple_of(r * LN_ROWS, LN_ROWS), LN_ROWS)
        y = _layer_norm(h_ref[rs, :], g_ref[...], be_ref[...])
        o_ref[rs, :] = (y * _sigmoid(y)).astype(o_ref.dtype)
        return carry
    lax.fori_loop(0, n_rows // LN_ROWS, row_chunk, 0, unroll=2)


def _conv_p_body(main_ref, prev_ref, w_ref, b_ref, g_ref, be_ref, o_ref, win_sc, h_sc, *, tt):
    first = pl.program_id(1) == 0
    win_sc[0:HALO, :] = jnp.where(first, 0.0, prev_ref[...])
    win_sc[HALO:HALO + tt, :] = main_ref[...]
    lead = HALO - (CONV_K - 1)
    for r0 in range(0, tt, CONV_ROWS):
        _conv_taps(win_sc, r0 + lead, w_ref, b_ref, h_sc, r0, CONV_ROWS)
    _ln_swish_rows(h_sc, g_ref, be_ref, o_ref, tt)


def _conv_prompt(glu, w, b, g, be, tt):
    n_t = SEQ // tt
    const = lambda bi, i: (0, 0)
    return pl.pallas_call(
        functools.partial(_conv_p_body, tt=tt),
        grid=(BATCH, n_t),
        in_specs=[pl.BlockSpec((tt, CONV_CH), lambda bi, i: (bi * n_t + i, 0)),
                  pl.BlockSpec((HALO, CONV_CH),
                               lambda bi, i: (jnp.maximum((bi * SEQ + i * tt) // HALO - 1, 0), 0)),
                  pl.BlockSpec((HALO, CONV_CH), const),
                  pl.BlockSpec((1, CONV_CH), const),
                  pl.BlockSpec((1, CONV_CH), const),
                  pl.BlockSpec((1, CONV_CH), const)],
        out_specs=pl.BlockSpec((tt, CONV_CH), lambda bi, i: (bi * n_t + i, 0)),
        out_shape=jax.ShapeDtypeStruct((P_TOK, CONV_CH), BF16),
        scratch_shapes=[pltpu.VMEM((HALO + tt, CONV_CH), F32), pltpu.VMEM((tt, CONV_CH), F32)],
        compiler_params=_cparams("parallel", "parallel"),
        name="conv_prompt",
    )(glu, glu, w, b, g, be)


def _conv_s_body(state_ref, new_ref, w_ref, b_ref, g_ref, be_ref, o_ref, h_sc, *, gb):
    hist = CONV_K - 1

    def chan_chunk(c, carry):
        cs = pl.ds(pl.multiple_of(c * CONV_LANES, CONV_LANES), CONV_LANES)
        bias = jnp.broadcast_to(b_ref[:, cs], (gb, CONV_LANES))
        acc = [bias] * DEC_SEQ
        for j in range(hist + DEC_SEQ):
            row = state_ref[j, :, cs] if j < hist else new_ref[:, j - hist, cs]
            for t in range(DEC_SEQ):
                k = j - t
                if 0 <= k < CONV_K:
                    acc[t] = acc[t] + w_ref[k:k + 1, cs] * row
        for t in range(DEC_SEQ):
            h_sc[:, t, cs] = acc[t]
        return carry

    lax.fori_loop(0, CONV_CH // CONV_LANES, chan_chunk, 0)
    pair = LN_ROWS // DEC_SEQ
    for r in range(gb // pair):
        h = h_sc[r * pair:(r + 1) * pair].reshape(LN_ROWS, CONV_CH)
        y = _layer_norm(h, g_ref[...], be_ref[...])
        o_ref[r * LN_ROWS:(r + 1) * LN_ROWS, :] = (y * _sigmoid(y)).astype(o_ref.dtype)


def _conv_sample(state_t, glu3, w, b, g, be, gb, layer):
    const = lambda i: (0, 0)
    return pl.pallas_call(
        functools.partial(_conv_s_body, gb=gb),
        grid=(DEC_BATCH // gb,),
        in_specs=[pl.BlockSpec((None, CONV_K - 1, gb, CONV_CH), lambda i: (layer, 0, i, 0)),
                  pl.BlockSpec((gb, DEC_SEQ, CONV_CH), lambda i: (i, 0, 0)),
                  pl.BlockSpec((HALO, CONV_CH), const),
                  pl.BlockSpec((1, CONV_CH), const),
                  pl.BlockSpec((1, CONV_CH), const),
                  pl.BlockSpec((1, CONV_CH), const)],
        out_specs=pl.BlockSpec((gb * DEC_SEQ, CONV_CH), lambda i: (i, 0)),
        out_shape=jax.ShapeDtypeStruct((S_TOK, CONV_CH), BF16),
        scratch_shapes=[pltpu.VMEM((gb, DEC_SEQ, CONV_CH), F32)],
        compiler_params=_cparams("parallel"),
        name="conv_sample",
    )(state_t, glu3, w, b, g, be)


def _q_body(qn_ref, wuq_ref, wuk_ref, cos_ref, sin_ref, *out_refs, sample):
    tm = qn_ref.shape[0]
    q = _dot(qn_ref[...], wuq_ref[...])
    nope_w = N_HEADS * QK_NOPE
    rope_w = N_HEADS * QK_ROPE
    qr = q[:, nope_w:]
    half = QK_ROPE // 2
    lane = lax.broadcasted_iota(jnp.int32, qr.shape, 1)
    ahead = pltpu.roll(qr, rope_w - half, 1)
    behind = pltpu.roll(qr, half, 1)
    swapped = jnp.where((lane % QK_ROPE) < half, ahead, behind)
    rot = qr * cos_ref[...] + swapped * sin_ref[...]
    for h in range(N_HEADS):
        nope = q[:, h * QK_NOPE:(h + 1) * QK_NOPE].astype(BF16)
        lat = _dot(nope, wuk_ref[h]) * MLA_SCALE
        rot_h = rot[:, h * QK_ROPE:(h + 1) * QK_ROPE] * MLA_SCALE
        if sample:
            ql_ref, qr_ref = out_refs
            ql_ref[:, h, :, :] = lat.reshape(tm // DEC_SEQ, DEC_SEQ, KV_LORA)
            qr_ref[:, h, :, :] = rot_h.reshape(tm // DEC_SEQ, DEC_SEQ, QK_ROPE)
        else:
            (qcat_ref,) = out_refs
            qcat_ref[0, h, :, :KV_LORA] = lat.astype(BF16)
            qcat_ref[0, h, :, KV_LORA:KV_LORA + QK_ROPE] = rot_h.astype(BF16)
            qcat_ref[0, h, :, KV_LORA + QK_ROPE:] = jnp.zeros((tm, QK_CAT - KV_LORA - QK_ROPE), BF16)


def _q_prompt(qn, wuq, wuk, cos_q, sin_q, tm):
    n_t = SEQ // tm
    c2 = lambda b, i: (0, 0)
    c3 = lambda b, i: (0, 0, 0)
    return pl.pallas_call(
        functools.partial(_q_body, sample=False),
        grid=(BATCH, n_t),
        in_specs=[pl.BlockSpec((tm, Q_LORA), lambda b, i: (b * n_t + i, 0)),
                  pl.BlockSpec((Q_LORA, N_HEADS * (QK_NOPE + QK_ROPE)), c2),
                  pl.BlockSpec((N_HEADS, QK_NOPE, KV_LORA), c3),
                  pl.BlockSpec((tm, N_HEADS * QK_ROPE), lambda b, i: (i, 0)),
                  pl.BlockSpec((tm, N_HEADS * QK_ROPE), lambda b, i: (i, 0))],
        out_specs=[pl.BlockSpec((1, N_HEADS, tm, QK_CAT), lambda b, i: (b, 0, i, 0))],
        out_shape=[jax.ShapeDtypeStruct((BATCH, N_HEADS, SEQ, QK_CAT), BF16)],
        compiler_params=_cparams("parallel", "parallel"),
        name="q_prompt",
    )(qn, wuq, wuk, cos_q, sin_q)[0]


def _q_sample(qn, wuq, wuk, cos_q, sin_q, tm):
    gb = tm // DEC_SEQ
    pos_blk = SEQ // tm
    c2 = lambda i: (0, 0)
    c3 = lambda i: (0, 0, 0)
    return pl.pallas_call(
        functools.partial(_q_body, sample=True),
        grid=(S_TOK // tm,),
        in_specs=[pl.BlockSpec((tm, Q_LORA), lambda i: (P_TOK // tm + i, 0)),
                  pl.BlockSpec((Q_LORA, N_HEADS * (QK_NOPE + QK_ROPE)), c2),
                  pl.BlockSpec((N_HEADS, QK_NOPE, KV_LORA), c3),
                  pl.BlockSpec((tm, N_HEADS * QK_ROPE), lambda i: (pos_blk, 0)),
                  pl.BlockSpec((tm, N_HEADS * QK_ROPE), lambda i: (pos_blk, 0))],
        out_specs=[pl.BlockSpec((gb, N_HEADS, DEC_SEQ, KV_LORA), lambda i: (i, 0, 0, 0)),
                   pl.BlockSpec((gb, N_HEADS, DEC_SEQ, QK_ROPE), lambda i: (i, 0, 0, 0))],
        out_shape=[jax.ShapeDtypeStruct((DEC_BATCH, N_HEADS, DEC_SEQ, KV_LORA), F32),
                   jax.ShapeDtypeStruct((DEC_BATCH, N_HEADS, DEC_SEQ, QK_ROPE), F32)],
        compiler_params=_cparams("parallel"),
        name="q_sample",
    )(qn, wuq, wuk, cos_q, sin_q)


def _attn_p_body(q_ref, k_ref, wuv_ref, o_ref, m_sc, l_sc, acc_sc, *, tq, tk):
    qi = pl.program_id(1)
    rows = N_HEADS * tq
    q = q_ref[0].reshape(rows, QK_CAT)
    m_sc[...] = jnp.full(m_sc.shape, NEG_INF, F32)
    l_sc[...] = jnp.zeros(l_sc.shape, F32)
    acc_sc[...] = jnp.zeros(acc_sc.shape, F32)
    n_full = (qi * tq) // tk

    def chunk(j, masked, width=tk):
        ks = pl.ds(pl.multiple_of(j * tk, tk), width)
        k = k_ref[ks, :]
        s = _dot_nt(q, k)
        if masked:
            qpos = qi * tq + lax.broadcasted_iota(jnp.int32, (tq, width), 0)
            kpos = j * tk + lax.broadcasted_iota(jnp.int32, (tq, width), 1)
            s = jnp.where((kpos <= qpos)[None], s.reshape(N_HEADS, tq, width), NEG_INF).reshape(rows, width)
        m_prev = m_sc[...]
        m_new = jnp.maximum(m_prev, jnp.max(s, axis=-1, keepdims=True))
        alpha = jnp.exp(m_prev - m_new)
        p = jnp.exp(s - m_new)
        l_sc[...] = alpha * l_sc[...] + jnp.sum(p, axis=-1, keepdims=True)
        acc_sc[...] = alpha * acc_sc[...] + _dot(p.astype(BF16), k[:, :KV_LORA])
        m_sc[...] = m_new

    def full_chunk(j, carry):
        chunk(j, False)
        return carry

    lax.fori_loop(0, n_full, full_chunk, 0)
    part = qi % (tk // tq)
    for r in range(tk // tq):
        @pl.when(part == r)
        def _(r=r):
            chunk(n_full, True, width=(r + 1) * tq)

    lat = acc_sc[...] / l_sc[...]
    for h in range(N_HEADS):
        lat_h = lat[h * tq:(h + 1) * tq, :].astype(BF16)
        o_ref[:, h * V_HEAD:(h + 1) * V_HEAD] = _dot(lat_h, wuv_ref[h]).astype(BF16)


def _attn_prompt(qcat, kcat, wuv, tq, tk):
    n_q = SEQ // tq
    rows = N_HEADS * tq
    return pl.pallas_call(
        functools.partial(_attn_p_body, tq=tq, tk=tk),
        grid=(BATCH, n_q),
        in_specs=[pl.BlockSpec((1, N_HEADS, tq, QK_CAT), lambda b, i: (b, 0, i, 0)),
                  pl.BlockSpec((SEQ, QK_CAT), lambda b, i: (b, 0)),
                  pl.BlockSpec((N_HEADS, KV_LORA, V_HEAD), lambda b, i: (0, 0, 0))],
        out_specs=pl.BlockSpec((tq, N_HEADS * V_HEAD), lambda b, i: (b * n_q + i, 0)),
        out_shape=jax.ShapeDtypeStruct((P_TOK, N_HEADS * V_HEAD), BF16),
        scratch_shapes=[pltpu.VMEM((rows, 1), F32), pltpu.VMEM((rows, 1), F32),
                        pltpu.VMEM((rows, KV_LORA), F32)],
        compiler_params=_cparams("parallel", "arbitrary"),
        name="attn_prompt",
    )(qcat, kcat, wuv)


def _attn_s_body(pt_ref, ql_ref, qr_ref, cn_ref, kn_ref, ckv_hbm, krt_hbm, o_ref,
                 m_sc, l_sc, acc_sc, kb_sc, krt_sc, ck_buf, kr_buf, sem, *, pages, nb, n_pages, layer):
    b = pl.program_id(0)
    g = pl.program_id(1)
    n_g = pl.num_programs(1)
    rows = N_HEADS * DEC_SEQ
    n_in = nb * pages
    step = b * n_g + g
    slot = step % 2

    def page_copies(step_b, step_g, buf_slot):
        copies = []
        for a in range(nb):
            for u in range(pages):
                i = a * pages + u
                page = pt_ref[(step_b * nb + a) * n_pages + step_g * pages + u]
                copies.append(pltpu.make_async_copy(ckv_hbm.at[layer, page], ck_buf.at[buf_slot, i],
                                                    sem.at[buf_slot, i]))
                copies.append(pltpu.make_async_copy(krt_hbm.at[layer, page], kr_buf.at[buf_slot, i],
                                                    sem.at[buf_slot, n_in + i]))
        return copies

    def start_all(copies):
        for n, cp in enumerate(copies):
            cp.start(priority=(n // 2) % 2)

    @pl.when(step == 0)
    def _():
        start_all(page_copies(b, g, slot))

    @pl.when(step + 1 < pl.num_programs(0) * n_g)
    def _():
        nxt = step + 1
        start_all(page_copies(nxt // n_g, nxt % n_g, 1 - slot))

    @pl.when(g == 0)
    def _():
        for a in range(nb):
            ql = ql_ref[a].reshape(rows, KV_LORA)
            qr = qr_ref[a].reshape(rows, QK_ROPE)
            cn = cn_ref[a]
            kn = kn_ref[a]
            s = _dot_nt(ql, cn) + _dot_nt(qr, kn)
            t = lax.broadcasted_iota(jnp.int32, s.shape, 0) % DEC_SEQ
            c = lax.broadcasted_iota(jnp.int32, s.shape, 1)
            s = jnp.where(c <= t, s, NEG_INF)
            m = jnp.max(s, axis=-1, keepdims=True)
            p = jnp.exp(s - m)
            m_sc[a] = m
            l_sc[a] = jnp.sum(p, axis=-1, keepdims=True)
            acc_sc[a] = _dot(p, cn)

    for cp in page_copies(b, g, slot):
        cp.wait()
    for a in range(nb):
        for u in range(pages):
            i = a * pages + u
            kb_sc[a, u * PAGE_SIZE:(u + 1) * PAGE_SIZE, :] = ck_buf[slot, i].astype(BF16)
            krt_sc[a, :, u * PAGE_SIZE:(u + 1) * PAGE_SIZE] = kr_buf[slot, i].astype(BF16)
    for a in range(nb):
        ql = ql_ref[a].reshape(rows, KV_LORA).astype(BF16)
        qr = qr_ref[a].reshape(rows, QK_ROPE).astype(BF16)
        kb = kb_sc[a]
        s = _dot_nt(ql, kb) + _dot(qr, krt_sc[a])
        m_prev = m_sc[a]
        m_new = jnp.maximum(m_prev, jnp.max(s, axis=-1, keepdims=True))
        alpha = jnp.exp(m_prev - m_new)
        p = jnp.exp(s - m_new)
        l_sc[a] = alpha * l_sc[a] + jnp.sum(p, axis=-1, keepdims=True)
        acc_sc[a] = alpha * acc_sc[a] + _dot(p.astype(BF16), kb)
        m_sc[a] = m_new

    @pl.when(g == pl.num_programs(1) - 1)
    def _():
        for a in range(nb):
            o_ref[a] = (acc_sc[a] / l_sc[a]).reshape(N_HEADS, DEC_SEQ, KV_LORA)


def _attn_sample(page_table, ql, qr, ckv_new, kr_new, cache_ckv, cache_krope_t, layer, pages, nb):
    n_pages = page_table.shape[1]
    n_g = n_pages // pages
    rows = N_HEADS * DEC_SEQ
    n_in = nb * pages
    per_b4 = lambda b, g, pt: (b, 0, 0, 0)
    per_b3 = lambda b, g, pt: (b, 0, 0)
    grid_spec = pltpu.PrefetchScalarGridSpec(
        num_scalar_prefetch=1,
        grid=(DEC_BATCH // nb, n_g),
        in_specs=[pl.BlockSpec((nb, N_HEADS, DEC_SEQ, KV_LORA), per_b4),
                  pl.BlockSpec((nb, N_HEADS, DEC_SEQ, QK_ROPE), per_b4),
                  pl.BlockSpec((nb, DEC_SEQ, KV_LORA), per_b3),
                  pl.BlockSpec((nb, DEC_SEQ, QK_ROPE), per_b3),
                  pl.BlockSpec(memory_space=pl.ANY),
                  pl.BlockSpec(memory_space=pl.ANY)],
        out_specs=pl.BlockSpec((nb, N_HEADS, DEC_SEQ, KV_LORA), per_b4),
        scratch_shapes=[pltpu.VMEM((nb, rows, 1), F32), pltpu.VMEM((nb, rows, 1), F32),
                        pltpu.VMEM((nb, rows, KV_LORA), F32),
                        pltpu.VMEM((nb, pages * PAGE_SIZE, KV_LORA), BF16),
                        pltpu.VMEM((nb, QK_ROPE, pages * PAGE_SIZE), BF16),
                        pltpu.VMEM((2, n_in, PAGE_SIZE, KV_LORA), F32),
                        pltpu.VMEM((2, n_in, QK_ROPE, PAGE_SIZE), F32),
                        pltpu.SemaphoreType.DMA((2, 2 * n_in))])
    return pl.pallas_call(
        functools.partial(_attn_s_body, pages=pages, nb=nb, n_pages=n_pages, layer=layer),
        grid_spec=grid_spec,
        out_shape=jax.ShapeDtypeStruct((DEC_BATCH, N_HEADS, DEC_SEQ, KV_LORA), F32),
        compiler_params=_cparams("arbitrary", "arbitrary"),
        name="attn_sample",
    )(page_table.reshape(-1), ql, qr, ckv_new, kr_new, cache_ckv, cache_krope_t)


def _vup_s_body(lat_ref, wuv_ref, o_ref):
    lat = lat_ref[...].reshape(S_TOK, KV_LORA).astype(BF16)
    o_ref[...] = _dot(lat, wuv_ref[0]).astype(BF16)


def _vup_sample(lat, wuv):
    return pl.pallas_call(
        _vup_s_body,
        grid=(N_HEADS,),
        in_specs=[pl.BlockSpec((DEC_BATCH, None, DEC_SEQ, KV_LORA), lambda h: (0, h, 0, 0)),
                  pl.BlockSpec((1, KV_LORA, V_HEAD), lambda h: (h, 0, 0))],
        out_specs=pl.BlockSpec((S_TOK, V_HEAD), lambda h: (0, h)),
        out_shape=jax.ShapeDtypeStruct((S_TOK, N_HEADS * V_HEAD), BF16),
        compiler_params=_cparams("parallel"),
        name="vup_sample",
    )(lat, wuv)


def _merge_body(vp_ref, vs_ref, hp_ref, hs_ref, x_ref, wo_ref, wp_ref, wga_ref, wgb_ref, ba_ref, bb_ref,
                o_ref, v_sc, h_sc, *, n_p):
    i = pl.program_id(0)
    j = pl.program_id(1)

    @pl.when((j == 0) & (i < n_p))
    def _():
        v_sc[...] = vp_ref[...]
        h_sc[...] = hp_ref[...]

    @pl.when((j == 0) & (i >= n_p))
    def _():
        v_sc[...] = vs_ref[...]
        h_sc[...] = hs_ref[...]

    x = x_ref[...]
    branch_a = _dot(v_sc[...], wo_ref[...])
    branch_b = _dot(h_sc[...], wp_ref[...])
    gate_a = _sigmoid(_dot(x, wga_ref[...]) + ba_ref[...])
    gate_b = _sigmoid(_dot(x, wgb_ref[...]) + bb_ref[...])
    o_ref[...] = (gate_a * branch_a + gate_b * branch_b).astype(BF16)


def _merge(v_p, v_s, hb_p, hb_s, xb, wo, wp, wg, bg, tm, tn):
    n_j = D_MODEL // tn
    n_p = P_TOK // tm
    row = lambda i, j: (i, 0)
    row_p = lambda i, j: (jnp.minimum(i, n_p - 1), 0)
    row_s = lambda i, j: (jnp.maximum(i - n_p, 0), 0)
    col = lambda i, j: (0, j)
    col_b = lambda i, j: (0, j + n_j)
    return pl.pallas_call(
        functools.partial(_merge_body, n_p=n_p),
        grid=(N_TOK // tm, n_j),
        in_specs=[pl.BlockSpec((tm, D_MODEL), row_p),
                  pl.BlockSpec((tm, D_MODEL), row_s),
                  pl.BlockSpec((tm, D_MODEL), row_p),
                  pl.BlockSpec((tm, D_MODEL), row_s),
                  pl.BlockSpec((tm, D_MODEL), row),
                  pl.BlockSpec((D_MODEL, tn), col),
                  pl.BlockSpec((D_MODEL, tn), col),
                  pl.BlockSpec((D_MODEL, tn), col),
                  pl.BlockSpec((D_MODEL, tn), col_b),
                  pl.BlockSpec((1, tn), col),
                  pl.BlockSpec((1, tn), col_b)],
        out_specs=pl.BlockSpec((tm, tn), lambda i, j: (i, j)),
        out_shape=jax.ShapeDtypeStruct((N_TOK, D_MODEL), BF16),
        scratch_shapes=[pltpu.VMEM((tm, D_MODEL), BF16), pltpu.VMEM((tm, D_MODEL), BF16)],
        compiler_params=_cparams("parallel", "arbitrary"),
        name="merge",
    )(v_p, v_s, hb_p, hb_s, xb, wo, wp, wg, wg, bg, bg)


def _mm_res_ln_body(a_ref, w_ref, res_ref, g_ref, b_ref, o_ref, ob_ref):
    z = DEEPNORM_ALPHA * res_ref[...] + _dot(a_ref[...], w_ref[...])
    y = _layer_norm(z, g_ref[...], b_ref[...])
    o_ref[...] = y
    ob_ref[...] = y.astype(BF16)


def _mm_res_ln(a, w, res, g, b, tm):
    k = a.shape[1]
    row = lambda i: (i, 0)
    const = lambda i: (0, 0)
    return pl.pallas_call(
        _mm_res_ln_body,
        grid=(N_TOK // tm,),
        in_specs=[pl.BlockSpec((tm, k), row),
                  pl.BlockSpec((k, D_MODEL), const),
                  pl.BlockSpec((tm, D_MODEL), row),
                  pl.BlockSpec((1, D_MODEL), const),
                  pl.BlockSpec((1, D_MODEL), const)],
        out_specs=[pl.BlockSpec((tm, D_MODEL), row), pl.BlockSpec((tm, D_MODEL), row)],
        out_shape=[jax.ShapeDtypeStruct((N_TOK, D_MODEL), F32),
                   jax.ShapeDtypeStruct((N_TOK, D_MODEL), BF16)],
        compiler_params=_cparams("parallel"),
        name="mm_res_ln",
    )(a, w, res, g, b)


def _xattn_tail(o, x_ref, wo_ref, g_ref, b_ref, wr_ref, br_ref, o_ref, ob_ref, lg_ref):
    z = DEEPNORM_ALPHA * x_ref[...] + _dot(o, wo_ref[...])
    y = _layer_norm(z, g_ref[...], b_ref[...])
    o_ref[...] = y
    ob_ref[...] = y.astype(BF16)
    lg_ref[...] = _dot(y.astype(BF16), wr_ref[...]) + br_ref[...]


def _xattn_p_body(x_ref, xb_ref, wq_ref, mk_ref, mv_ref, wo_ref, g_ref, b_ref, wr_ref, br_ref,
                  o_ref, ob_ref, lg_ref):
    q = _dot(xb_ref[...], wq_ref[...])
    heads = []
    for h in range(XA_HEADS):
        hs = slice(h * XA_HEAD_DIM, (h + 1) * XA_HEAD_DIM)
        kh = mk_ref[0, :, hs].astype(BF16)
        vh = mv_ref[0, :, hs].astype(BF16)
        s = _dot_nt(q[:, hs].astype(BF16), kh) * XA_SCALE
        e = jnp.exp(s - jnp.max(s, axis=-1, keepdims=True))
        p = e / jnp.sum(e, axis=-1, keepdims=True)
        heads.append(_dot(p.astype(BF16), vh))
    o = jnp.concatenate(heads, axis=-1).astype(BF16)
    _xattn_tail(o, x_ref, wo_ref, g_ref, b_ref, wr_ref, br_ref, o_ref, ob_ref, lg_ref)


def _xattn_s_body(x_ref, xb_ref, wq_ref, mk_ref, mv_ref, wo_ref, g_ref, b_ref, wr_ref, br_ref,
                  o_ref, ob_ref, lg_ref, *, gb):
    q = _dot(xb_ref[...], wq_ref[...])
    rows = XA_HEADS * DEC_SEQ
    keys = N_MEM * XA_HEADS
    row_head = lax.broadcasted_iota(jnp.int32, (rows, keys), 0) // DEC_SEQ
    key_head = lax.broadcasted_iota(jnp.int32, (rows, keys), 1) % XA_HEADS
    own = row_head == key_head
    outs = []
    for gi in range(gb):
        qg = q[gi * DEC_SEQ:(gi + 1) * DEC_SEQ, :]
        q4 = jnp.concatenate([qg[:, h * XA_HEAD_DIM:(h + 1) * XA_HEAD_DIM] for h in range(XA_HEADS)],
                             axis=0).astype(BF16)
        k = mk_ref[gi].astype(BF16)
        v = mv_ref[gi].astype(BF16)
        s = jnp.where(own, _dot_nt(q4, k) * XA_SCALE, NEG_INF)
        e = jnp.exp(s - jnp.max(s, axis=-1, keepdims=True))
        p = e / jnp.sum(e, axis=-1, keepdims=True)
        o4 = _dot(p.astype(BF16), v)
        outs.append(jnp.concatenate([o4[h * DEC_SEQ:(h + 1) * DEC_SEQ, :] for h in range(XA_HEADS)], axis=1))
    o = jnp.concatenate(outs, axis=0).astype(BF16)
    _xattn_tail(o, x_ref, wo_ref, g_ref, b_ref, wr_ref, br_ref, o_ref, ob_ref, lg_ref)


def _xattn(body, x, xb, wq, mk, mv, wo, g, b, wr, br, *, row0, n_rows, tm, mem_spec, name):
    blk0 = row0 // tm
    hd = XA_HEADS * XA_HEAD_DIM
    row = lambda i: (blk0 + i, 0)
    out_row = lambda i: (i, 0)
    const = lambda i: (0, 0)
    return pl.pallas_call(
        body,
        grid=(n_rows // tm,),
        in_specs=[pl.BlockSpec((tm, D_MODEL), row),
                  pl.BlockSpec((tm, D_MODEL), row),
                  pl.BlockSpec((D_MODEL, hd), const),
                  mem_spec,
                  mem_spec,
                  pl.BlockSpec((hd, D_MODEL), const),
                  pl.BlockSpec((1, D_MODEL), const),
                  pl.BlockSpec((1, D_MODEL), const),
                  pl.BlockSpec((D_MODEL, ROUTER_LANES), const),
                  pl.BlockSpec((1, ROUTER_LANES), const)],
        out_specs=[pl.BlockSpec((tm, D_MODEL), out_row), pl.BlockSpec((tm, D_MODEL), out_row),
                   pl.BlockSpec((tm, ROUTER_LANES), out_row)],
        out_shape=[jax.ShapeDtypeStruct((n_rows, D_MODEL), F32),
                   jax.ShapeDtypeStruct((n_rows, D_MODEL), BF16),
                   jax.ShapeDtypeStruct((n_rows, ROUTER_LANES), F32)],
        compiler_params=_cparams("parallel"),
        name=name,
    )(x, xb, wq, mk, mv, wo, g, b, wr, br)


def _route(logits):
    lane = lax.broadcasted_iota(jnp.int32, logits.shape, 1)
    big = jnp.int32(ROUTER_LANES)
    gl = jnp.where(lane < N_GROUPS, logits, NEG_INF)
    gmax = jnp.max(gl, axis=-1, keepdims=True)
    g_sel = jnp.min(jnp.where(gl == gmax, lane, big), axis=-1, keepdims=True)
    g_w = 1.0 / jnp.sum(jnp.exp(gl - gmax), axis=-1, keepdims=True)
    lo = N_GROUPS + EXP_PER_GROUP * g_sel
    el = jnp.where((lane >= lo) & (lane < lo + EXP_PER_GROUP), logits, NEG_INF)
    v1 = jnp.max(el, axis=-1, keepdims=True)
    i1 = jnp.min(jnp.where(el == v1, lane, big), axis=-1, keepdims=True)
    el2 = jnp.where(lane == i1, NEG_INF, el)
    v2 = jnp.max(el2, axis=-1, keepdims=True)
    i2 = jnp.min(jnp.where(el2 == v2, lane, big), axis=-1, keepdims=True)
    e2 = jnp.exp(v2 - v1)
    w1 = g_w / (1.0 + e2)
    w2 = g_w * e2 / (1.0 + e2)
    return jnp.where(lane == i1, w1, 0.0) + jnp.where(lane == i2, w2, 0.0)


MOE_TILE = 768
MOE_CHUNK = 128


def _moe_body(lg_ref, xb_ref, tri_ref, wg_ref, wu_ref, wd_ref, o_ref, comb_sc, combt_sc, rank_sc, rankt_sc):
    e = pl.program_id(1)
    tm = xb_ref.shape[0]

    @pl.when(e == 0)
    def _():
        comb = _route(lg_ref[...])
        combt = comb.T
        comb_sc[...] = comb
        combt_sc[...] = combt
        chosen = jnp.where(combt > 0.0, 1.0, 0.0).astype(BF16)
        rank = _dot(chosen, tri_ref[...])
        rank_sc[...] = rank
        rankt_sc[...] = rank.T
        o_ref[...] = jnp.zeros(o_ref.shape, F32)

    le = N_GROUPS + e
    w_row = combt_sc[pl.ds(le, 1), :]
    r_row = rank_sc[pl.ds(le, 1), :]
    lane = lax.broadcasted_iota(jnp.int32, (tm, ROUTER_LANES), 1)
    w_col = jnp.sum(jnp.where(lane == le, comb_sc[...], 0.0), axis=-1, keepdims=True)
    r_col = jnp.sum(jnp.where(lane == le, rankt_sc[...], 0.0), axis=-1, keepdims=True)
    count = jnp.sum(jnp.where(w_row > 0.0, 1.0, 0.0)).astype(jnp.int32)
    slot_r = lax.broadcasted_iota(jnp.int32, (MOE_CHUNK, tm), 0).astype(F32)
    slot_c = lax.broadcasted_iota(jnp.int32, (tm, MOE_CHUNK), 1).astype(F32)

    def chunk(ci, carry):
        off = (ci * MOE_CHUNK).astype(F32)
        hit = (r_row == slot_r + off) & (w_row > 0.0)
        xg = _dot(jnp.where(hit, 1.0, 0.0).astype(BF16), xb_ref[...]).astype(BF16)
        cg = jnp.sum(jnp.where(hit, w_row, 0.0), axis=-1, keepdims=True)
        gate = _dot(xg, wg_ref[...])
        up = _dot(xg, wu_ref[...])
        hid = (gate * _sigmoid(gate)) * up * cg
        y = _dot(hid.astype(BF16), wd_ref[...]).astype(BF16)
        hit_t = (r_col == slot_c + off) & (w_col > 0.0)
        o_ref[...] += _dot(jnp.where(hit_t, 1.0, 0.0).astype(BF16), y)
        return carry

    lax.fori_loop(0, (count + MOE_CHUNK - 1) // MOE_CHUNK, chunk, 0)


def _moe(logits, xb, tri, wg, wu, wd, layer):
    tm = MOE_TILE
    row = lambda i, e: (i, 0)
    const = lambda i, e: (0, 0)
    expert = lambda i, e: (layer, e, 0, 0)
    return pl.pallas_call(
        _moe_body,
        grid=(N_TOK // tm, N_EXPERTS),
        in_specs=[pl.BlockSpec((tm, ROUTER_LANES), row),
                  pl.BlockSpec((tm, D_MODEL), row),
                  pl.BlockSpec((tm, tm), const),
                  pl.BlockSpec((None, None, D_MODEL, D_EXPERT), expert),
                  pl.BlockSpec((None, None, D_MODEL, D_EXPERT), expert),
                  pl.BlockSpec((None, None, D_EXPERT, D_MODEL), expert)],
        out_specs=pl.BlockSpec((tm, D_MODEL), row),
        out_shape=jax.ShapeDtypeStruct((N_TOK, D_MODEL), F32),
        scratch_shapes=[pltpu.VMEM((tm, ROUTER_LANES), F32), pltpu.VMEM((ROUTER_LANES, tm), F32),
                        pltpu.VMEM((ROUTER_LANES, tm), F32), pltpu.VMEM((tm, ROUTER_LANES), F32)],
        compiler_params=_cparams("parallel", "arbitrary"),
        name="moe",
    )(logits, xb, tri, wg, wu, wd)


def _add_ln_body(xp_ref, xs_ref, acc_ref, g_ref, b_ref, o_ref, ob_ref, *, n_p):
    x = jnp.where(pl.program_id(0) < n_p, xp_ref[...], xs_ref[...])
    y = _layer_norm(DEEPNORM_ALPHA * x + acc_ref[...], g_ref[...], b_ref[...])
    o_ref[...] = y
    ob_ref[...] = y.astype(BF16)


def _add_ln(x_p, x_s, acc, g, b, tm):
    n_p = P_TOK // tm
    row = lambda i: (i, 0)
    const = lambda i: (0, 0)
    return pl.pallas_call(
        functools.partial(_add_ln_body, n_p=n_p),
        grid=(N_TOK // tm,),
        in_specs=[pl.BlockSpec((tm, D_MODEL), lambda i: (jnp.minimum(i, n_p - 1), 0)),
                  pl.BlockSpec((tm, D_MODEL), lambda i: (jnp.maximum(i - n_p, 0), 0)),
                  pl.BlockSpec((tm, D_MODEL), row),
                  pl.BlockSpec((1, D_MODEL), const),
                  pl.BlockSpec((1, D_MODEL), const)],
        out_specs=[pl.BlockSpec((tm, D_MODEL), row), pl.BlockSpec((tm, D_MODEL), row)],
        out_shape=[jax.ShapeDtypeStruct((N_TOK, D_MODEL), F32),
                   jax.ShapeDtypeStruct((N_TOK, D_MODEL), BF16)],
        compiler_params=_cparams("parallel"),
        name="add_ln",
    )(x_p, x_s, acc, g, b)


def _rotary(pos):
    half = QK_ROPE // 2
    inv = ROPE_THETA ** (-jnp.arange(half, dtype=F32) / half)
    ang = pos.astype(F32)[:, None] * inv[None, :]
    cos, sin = jnp.cos(ang), jnp.sin(ang)
    return jnp.concatenate([cos, cos], axis=-1), jnp.concatenate([-sin, sin], axis=-1)


TM_PROJ = 512
TM_Q = 256
TQ_ATTN = 128
TK_ATTN = 512
TT_CONV = 128
GB_CONV = 8
GB_XA = 8
PAGES_PER_STEP = 16
ELEMS_PER_STEP = 2
TM_LN = 256
TM_ADD_LN = 512


def kernel(x_prompt, x_sample, mem_prompt, cache_ckv, cache_krope, cache_mem_k, cache_mem_v, state_conv, page_table, w_in, b_gate, q_norm_g, w_uq, kv_norm_g, w_uk, w_uv, w_o_mla, conv_w, conv_b, conv_ln_g, conv_ln_b, w_pw2, w_out, xa_wq, xa_wk, xa_wv, xa_wo, router_group_w, router_group_b, router_expert_w, router_expert_b, w_gate_e, w_up_e, w_down_e, ln_g, ln_b):
    n_pages = page_table.shape[1]
    past_len = n_pages * PAGE_SIZE
    hd = XA_HEADS * XA_HEAD_DIM

    pos_rows = max(TM_PROJ, TM_Q)
    pos = jnp.concatenate([jnp.arange(SEQ, dtype=jnp.int32),
                           past_len + (jnp.arange(pos_rows, dtype=jnp.int32) % DEC_SEQ)])
    cos64, sin64 = _rotary(pos)
    cos_q = jnp.tile(cos64, (1, N_HEADS))
    sin_q = jnp.tile(sin64, (1, N_HEADS))
    n_pos_k = SEQ + TM_PROJ
    n_pos_q = SEQ + TM_Q

    x = jnp.concatenate([x_prompt.reshape(P_TOK, D_MODEL), x_sample.reshape(S_TOK, D_MODEL)], axis=0)
    xb = x.astype(BF16)

    w_mem = jnp.concatenate([xa_wk.reshape(DEPTH, D_MODEL, hd), xa_wv.reshape(DEPTH, D_MODEL, hd)], axis=-1)
    w_mem = jnp.moveaxis(w_mem, 0, 1).reshape(D_MODEL, DEPTH * 2 * hd).astype(BF16)
    mem_kv = _mm(mem_prompt.reshape(BATCH * N_MEM, D_MODEL).astype(BF16), w_mem, 512, 512)
    mem_kv = mem_kv.reshape(BATCH, N_MEM, DEPTH, 2, hd)

    cache_krope_t = jnp.swapaxes(cache_krope, 2, 3)
    mem_k_rows = cache_mem_k.reshape(DEPTH, DEC_BATCH, N_MEM * XA_HEADS, XA_HEAD_DIM)
    mem_v_rows = cache_mem_v.reshape(DEPTH, DEC_BATCH, N_MEM * XA_HEADS, XA_HEAD_DIM)
    tri = jnp.triu(jnp.ones((MOE_TILE, MOE_TILE), F32), k=1).astype(BF16)
    state_t = jnp.transpose(state_conv, (0, 2, 1, 3))
    w_gate_b, w_up_b, w_down_b =w_gate_e.astype(BF16), w_up_e.astype(BF16), w_down_e.astype(BF16)

    outs ={k: [] for k in ("ckv_p", "kr_p", "mk_p", "mv_p", "cv_p", "ckv_s", "kr_s", "cv_s")}
    for l in range(DEPTH):
        wl = w_in[l]
        w1 = wl[:, :C1].astype(BF16)
        wa = wl[:, C1:C1 + CONV_CH].astype(BF16)
        wb = wl[:, C1 + CONV_CH:C1 + 2 * CONV_CH].astype(BF16)
        wg = wl[:, C1 + 2 * CONV_CH:].astype(BF16)
        wuq = jnp.concatenate([w_uq[l][:, :, :QK_NOPE].reshape(Q_LORA, N_HEADS * QK_NOPE),
                               w_uq[l][:, :, QK_NOPE:].reshape(Q_LORA, N_HEADS * QK_ROPE)], axis=-1).astype(BF16)
        wuk = jnp.transpose(w_uk[l], (1, 2, 0)).astype(BF16)
        wuv = jnp.transpose(w_uv[l], (1, 0, 2)).astype(BF16)
        wo = w_o_mla[l].reshape(N_HEADS * V_HEAD, D_MODEL).astype(BF16)
        wp = w_pw2[l].astype(BF16)
        wout = w_out[l].astype(BF16)
        wxq = xa_wq[l].reshape(D_MODEL, hd).astype(BF16)
        wxo = xa_wo[l].reshape(hd, D_MODEL).astype(BF16)
        wr = jnp.concatenate([router_group_w[l], router_expert_w[l]], axis=-1)
        wr = jnp.pad(wr, ((0, 0), (0, ROUTER_LANES - wr.shape[1]))).astype(BF16)
        br = jnp.concatenate([router_group_b[l], router_expert_b[l]])
        br = jnp.pad(br, (0, ROUTER_LANES - br.shape[0]))[None, :]
        conv_w_l = jnp.pad(conv_w[l], ((0, HALO - CONV_K), (0, 0)))
        row = lambda v: v[None, :]

        qn, ckv, kr, kcat = _proj1(xb, w1, row(q_norm_g[l]), row(kv_norm_g[l]),
                                        cos64[:n_pos_k], sin64[:n_pos_k], TM_PROJ)
        glu = _glu(xb, wa, wb, 512, 512)

        hb_p = _conv_prompt(glu, conv_w_l, row(conv_b[l]), row(conv_ln_g[l]), row(conv_ln_b[l]), TT_CONV)
        glu_s = glu[P_TOK:].reshape(DEC_BATCH, DEC_SEQ, CONV_CH)
        hb_s = _conv_sample(state_t, glu_s, conv_w_l, row(conv_b[l]), row(conv_ln_g[l]),
                            row(conv_ln_b[l]), GB_CONV, l)

        qcat_p = _q_prompt(qn, wuq, wuk, cos_q[:n_pos_q], sin_q[:n_pos_q], TM_Q)
        ql_s, qr_s = _q_sample(qn, wuq, wuk, cos_q[:n_pos_q], sin_q[:n_pos_q], TM_Q)
        v_p = _attn_prompt(qcat_p, kcat, wuv, TQ_ATTN, TK_ATTN)
        ckv_s = ckv[P_TOK:].reshape(DEC_BATCH, DEC_SEQ, KV_LORA)
        kr_s = kr[P_TOK:].reshape(DEC_BATCH, DEC_SEQ, QK_ROPE)
        lat_s = _attn_sample(page_table, ql_s, qr_s, ckv_s, kr_s, cache_ckv, cache_krope_t, l,
                             PAGES_PER_STEP, ELEMS_PER_STEP)
        v_s = _vup_sample(lat_s, wuv)

        merged = _merge(v_p, v_s, hb_p, hb_s, xb, wo, wp, wg, row(b_gate[l]), 512, 512)
        x1, x1b = _mm_res_ln(merged, wout, x, row(ln_g[l, 0]), row(ln_b[l, 0]), TM_LN)

        mk_p = mem_kv[:, :, l, 0]
        mv_p = mem_kv[:, :, l, 1]
        xa_args = (wxo, row(ln_g[l, 1]), row(ln_b[l, 1]), wr, br)
        x2_p, x2b_p, lg_p = _xattn(
            _xattn_p_body, x1, x1b, wxq, mk_p, mv_p, *xa_args, row0=0, n_rows=P_TOK, tm=512,
            mem_spec=pl.BlockSpec((1, N_MEM, hd), lambda i: (i // (SEQ // 512), 0, 0)), name="xattn_prompt")
        x2_s, x2b_s, lg_s = _xattn(
            functools.partial(_xattn_s_body, gb=GB_XA), x1, x1b, wxq, mem_k_rows, mem_v_rows, *xa_args,
            row0=P_TOK, n_rows=S_TOK, tm=GB_XA * DEC_SEQ,
            mem_spec=pl.BlockSpec((None, GB_XA, N_MEM * XA_HEADS, XA_HEAD_DIM), lambda i, l=l: (l, i, 0, 0)),
            name="xattn_sample")

        moe_out = _moe(jnp.concatenate([lg_p, lg_s], axis=0), jnp.concatenate([x2b_p, x2b_s], axis=0), tri,
                       w_gate_b, w_up_b, w_down_b, l)
        x, xb = _add_ln(x2_p, x2_s, moe_out, row(ln_g[l, 2]), row(ln_b[l, 2]), TM_ADD_LN)

        outs["ckv_p"].append(ckv[:P_TOK].reshape(BATCH, SEQ, KV_LORA))
        outs["kr_p"].append(kr[:P_TOK].reshape(BATCH, SEQ, QK_ROPE))
        outs["mk_p"].append(mk_p.reshape(BATCH, N_MEM, XA_HEADS, XA_HEAD_DIM))
        outs["mv_p"].append(mv_p.reshape(BATCH, N_MEM, XA_HEADS, XA_HEAD_DIM))
        outs["cv_p"].append(jnp.stack([glu[(bi + 1) * SEQ - (CONV_K - 1):(bi + 1) * SEQ] for bi in range(BATCH)]))
        outs["ckv_s"].append(ckv_s)
        outs["kr_s"].append(kr_s)
        outs["cv_s"].append(jnp.concatenate([state_conv[l][:, DEC_SEQ:], glu_s], axis=1))

    return (x[:P_TOK].reshape(BATCH, SEQ, D_MODEL), x[P_TOK:].reshape(DEC_BATCH, DEC_SEQ, D_MODEL),
            jnp.stack(outs["ckv_p"]), jnp.stack(outs["kr_p"]), jnp.stack(outs["mk_p"]), jnp.stack(outs["mv_p"]),
            jnp.stack(outs["cv_p"]), jnp.stack(outs["ckv_s"]), jnp.stack(outs["kr_s"]), jnp.stack(outs["cv_s"]))
```
